```python
import math
import jax, jax.numpy as jnp
from jax import lax
import numpy as np

D_MODEL = 4096
BATCH = 4
SEQ = 2048
DEPTH = 4
DEC_BATCH = 8
DEC_SEQ = 1
PAST_LEN = 8192
PAGE_SIZE = 128

HEAD_DIM = 128
DIL_GROUPS = ((128, 1), (512, 4), (2048, 16))
N_GROUPS = len(DIL_GROUPS)
H_A = 10
H_B = D_MODEL // HEAD_DIM
ROT_DIM = HEAD_DIM // 4
ROPE_THETA = 500000.0
D_FF = -(-8 * D_MODEL // (3 * 256)) * 256
QB = 128
N_MIXERS = 2
N_A_LAYERS = (DEPTH + 1) // 2
N_B_LAYERS = DEPTH // 2
ALPHA = (2.0 * DEPTH) ** 0.25
BETA = (8.0 * DEPTH) ** -0.25
LN_EPS = 1e-5
SB_BIAS_INIT = -6.0

kernel_name = "dilated_stickbreaking_hybrid_step"


def layer_norm(x, g, b):
    xf = x.astype(jnp.float32)
    mu = jnp.mean(xf, axis=-1, keepdims=True)
    var = jnp.mean(jnp.square(xf - mu), axis=-1, keepdims=True)
    return ((xf - mu) * lax.rsqrt(var + LN_EPS) * g + b).astype(x.dtype)


def rope(x, pos):
    half = ROT_DIM // 2
    inv_freq = ROPE_THETA ** (-jnp.arange(half, dtype=jnp.float32) / half)
    ang = pos.astype(jnp.float32)[:, None] * inv_freq[None, :]
    shape = (1, pos.shape[0]) + (1,) * (x.ndim - 3) + (half,)
    cos = jnp.cos(ang).reshape(shape)
    sin = jnp.sin(ang).reshape(shape)
    xr = x[..., :ROT_DIM].astype(jnp.float32)
    x1, x2 = xr[..., :half], xr[..., half:]
    rot = jnp.concatenate([x1 * cos - x2 * sin, x2 * cos + x1 * sin], axis=-1).astype(x.dtype)
    return jnp.concatenate([rot, x[..., ROT_DIM:]], axis=-1)


def dilated_prompt(q, k, v, window, dilation):
    b, s, h, dh = q.shape
    n = window // dilation
    L = s // dilation
    nblk = -(-L // QB)
    lp = nblk * QB
    n_prev = -(-n // QB)
    kb_len = (n_prev + 1) * QB

    def strided(x):
        x = x.reshape(b, L, dilation, h, dh).transpose(0, 2, 1, 3, 4)
        return jnp.pad(x, ((0, 0), (0, 0), (0, lp - L), (0, 0), (0, 0)))

    def band(x):
        xp = jnp.pad(strided(x), ((0, 0), (0, 0), (n_prev * QB, 0), (0, 0), (0, 0)))
        return jnp.concatenate(
            [xp[:, :, p * QB: p * QB + lp].reshape(b, dilation, nblk, QB, h, dh) for p in range(n_prev + 1)], axis=3)

    qs = strided(q).reshape(b, dilation, nblk, QB, h, dh)
    kb, vb = band(k), band(v)
    sc = jnp.einsum('brcqhd,brckhd->brchqk', qs, kb, preferred_element_type=jnp.float32) / math.sqrt(dh)
    qq = jnp.arange(QB)[:, None]
    kk = jnp.arange(kb_len)[None, :]
    dist = qq + n_prev * QB - kk
    kpos = jnp.arange(nblk)[:, None, None] * QB - n_prev * QB + kk[None]
    mask = (dist >= 0) & (dist <= n) & (kpos >= 0)
    sc = jnp.where(mask[None, None, :, None], sc, -jnp.inf)
    mx = jnp.max(sc, axis=-1, keepdims=True)
    p = jnp.exp(sc - mx)
    den = jnp.sum(p, axis=-1)
    o = jnp.einsum('brchqk,brckhd->brcqhd', p, vb, preferred_element_type=jnp.float32)
    o = o / den.transpose(0, 1, 2, 4, 3)[..., None]
    lse = (mx[..., 0] + jnp.log(den)).transpose(0, 1, 2, 4, 3)

    def unstride(x):
        x = x.reshape((b, dilation, lp) + x.shape[4:])[:, :, :L]
        x = jnp.swapaxes(x, 1, 2)
        return x.reshape((b, s) + x.shape[3:])

    return unstride(o), unstride(lse)


def dilated_sample(q, kc, vc, window, dilation, lbuf):
    t, dh = q.shape[1], q.shape[3]
    n = window // dilation
    idx = lbuf + jnp.arange(t)[:, None] - dilation * jnp.arange(n + 1)[None, :]
    valid = idx >= 0
    idc = jnp.maximum(idx, 0)
    kg = kc[:, idc]
    vg = vc[:, idc]
    sc = jnp.einsum('bthd,btmhd->bhtm', q, kg, preferred_element_type=jnp.float32) / math.sqrt(dh)
    sc = jnp.where(valid[None, None], sc, -jnp.inf)
    mx = jnp.max(sc, axis=-1, keepdims=True)
    p = jnp.exp(sc - mx)
    den = jnp.sum(p, axis=-1)
    o = jnp.einsum('bhtm,btmhd->bthd', p, vg, preferred_element_type=jnp.float32)
    o = o / den.transpose(0, 2, 1)[..., None]
    lse = (mx[..., 0] + jnp.log(den)).transpose(0, 2, 1)
    return o, lse


def combine_groups(outs, lses):
    w = jax.nn.softmax(jnp.stack(lses, axis=0), axis=0)
    return jnp.sum(w[..., None] * jnp.stack(outs, axis=0), axis=0)


def project_a(x, w_qkv, pos):
    b, t, _ = x.shape
    qkv = jnp.einsum('btd,de->bte', x, w_qkv).reshape(b, t, 3, N_GROUPS, H_A, HEAD_DIM)
    return rope(qkv[:, :, 0], pos), rope(qkv[:, :, 1], pos), qkv[:, :, 2]


def mixer_a_prompt(x, w_qkv, w_o):
    b, s, _ = x.shape
    q, k, v = project_a(x, w_qkv, jnp.arange(s))
    outs, lses, states = [], [], []
    for g, (window, dilation) in enumerate(DIL_GROUPS):
        o, l = dilated_prompt(q[:, :, g], k[:, :, g], v[:, :, g], window, dilation)
        outs.append(o)
        lses.append(l)
        keep = min(window, s)
        states.append(jnp.stack([k[:, s - keep:, g], v[:, s - keep:, g]], axis=2))
    o = combine_groups(outs, lses).reshape(b, s, H_A * HEAD_DIM).astype(x.dtype)
    return jnp.einsum('bte,ed->btd', o, w_o), states


def mixer_a_sample(x, bufs, w_qkv, w_o):
    b, t, _ = x.shape
    q, k, v = project_a(x, w_qkv, PAST_LEN + jnp.arange(t))
    outs, lses, states = [], [], []
    for g, (window, dilation) in enumerate(DIL_GROUPS):
        buf = bufs[g]
        lbuf = buf.shape[1]
        kvc = jnp.concatenate([buf, jnp.stack([k[:, :, g], v[:, :, g]], axis=2)], axis=1)
        o, l = dilated_sample(q[:, :, g], kvc[:, :, 0], kvc[:, :, 1], window, dilation, lbuf)
        outs.append(o)
        lses.append(l)
        states.append(kvc[:, t:])
    o = combine_groups(outs, lses).reshape(b, t, H_A * HEAD_DIM).astype(x.dtype)
    return jnp.einsum('bte,ed->btd', o, w_o), states


def stick_breaking_weights(sc, mask):
    log_keep = jnp.where(mask, jax.nn.log_sigmoid(-sc), 0.0)
    after = lax.cumsum(log_keep, axis=3, reverse=True) - log_keep
    return jnp.where(mask, jnp.exp(jax.nn.log_sigmoid(sc) + after), 0.0)


def project_b(x, w_qkv):
    b, t, _ = x.shape
    qkv = jnp.einsum('btd,de->bte', x, w_qkv).reshape(b, t, 3, H_B, HEAD_DIM)
    return qkv[:, :, 0], qkv[:, :, 1], qkv[:, :, 2]


def mixer_b_prompt(x, w_qkv, w_o, sb_bias):
    b, s, _ = x.shape
    q, k, v = project_b(x, w_qkv)
    scale = 1.0 / math.sqrt(HEAD_DIM)
    bias = sb_bias.astype(jnp.float32)[None, :, None, None]
    outs = []
    for c in range(s // QB):
        start, end = c * QB, (c + 1) * QB
        sc = jnp.einsum('bqhd,bkhd->bhqk', q[:, start:end], k[:, :end], preferred_element_type=jnp.float32) * scale + bias
        mask = jnp.arange(end)[None, :] < jnp.arange(start, end)[:, None]
        a = stick_breaking_weights(sc, mask)
        outs.append(jnp.einsum('bhqk,bkhd->bqhd', a, v[:, :end], preferred_element_type=jnp.float32))
    o = jnp.concatenate(outs, axis=1).reshape(b, s, H_B * HEAD_DIM).astype(x.dtype)
    return jnp.einsum('bte,ed->btd', o, w_o), k, v


def mixer_b_sample(x, k_pages, v_pages, w_qkv, w_o, sb_bias):
    b, t, _ = x.shape
    q, k, v = project_b(x, w_qkv)
    past_len = k_pages.shape[1] * k_pages.shape[2]
    kp = k_pages.reshape(b, past_len, H_B, HEAD_DIM)
    vp = v_pages.reshape(b, past_len, H_B, HEAD_DIM)
    scale = 1.0 / math.sqrt(HEAD_DIM)
    bias = sb_bias.astype(jnp.float32)[None, :, None, None]
    sc = jnp.concatenate([
        jnp.einsum('bqhd,bkhd->bhqk', q, kp, preferred_element_type=jnp.float32),
        jnp.einsum('bqhd,bkhd->bhqk', q, k, preferred_element_type=jnp.float32)], axis=-1) * scale + bias
    mask = jnp.arange(past_len + t)[None, :] < (past_len + jnp.arange(t))[:, None]
    a = stick_breaking_weights(sc, mask)
    o = (jnp.einsum('bhqk,bkhd->bqhd', a[..., :past_len], vp, preferred_element_type=jnp.float32)
         + jnp.einsum('bhqk,bkhd->bqhd', a[..., past_len:], v, preferred_element_type=jnp.float32))
    o = o.reshape(b, t, H_B * HEAD_DIM).astype(x.dtype)
    return jnp.einsum('bte,ed->btd', o, w_o), k, v


def swiglu(x, w_in, w_out):
    h = jnp.einsum('btd,df->btf', x, w_in)
    gate, up = jnp.split(h, 2, axis=-1)
    return jnp.einsum('btf,fd->btd', jax.nn.silu(gate) * up, w_out)


def setup_inputs(seed: int = 0) -> dict:
    key = jax.random.key(seed)
    ks = jax.random.split(key, 20)
    n_pages = PAST_LEN // PAGE_SIZE
    n_used = DEC_BATCH * n_pages
    n_pool = n_used + n_used // 4
    f32 = jnp.float32
    nrm = lambda k, shape, s=1.0: jax.random.normal(k, shape, f32) * s
    caches = {}
    for g, (window, _) in enumerate(DIL_GROUPS):
        caches['cache_a_g%d' % g] = nrm(ks[2 + g], (N_A_LAYERS, DEC_BATCH, min(window, PAST_LEN), 2, H_A, HEAD_DIM))
    page_table = jax.random.permutation(ks[7], n_pool)[:n_used].reshape(DEC_BATCH, n_pages).astype(jnp.int32)
    dqa = 3 * N_GROUPS * H_A * HEAD_DIM
    dqb = 3 * H_B * HEAD_DIM
    return {
        'x_prompt': nrm(ks[0], (BATCH, SEQ, D_MODEL)),
        'x_sample': nrm(ks[1], (DEC_BATCH, DEC_SEQ, D_MODEL)),
        'cache_a_g0': caches['cache_a_g0'],
        'cache_a_g1': caches['cache_a_g1'],
        'cache_a_g2': caches['cache_a_g2'],
        'cache_b_k': nrm(ks[5], (N_B_LAYERS, n_pool, PAGE_SIZE, H_B, HEAD_DIM)),
        'cache_b_v': nrm(ks[6], (N_B_LAYERS, n_pool, PAGE_SIZE, H_B, HEAD_DIM)),
        'page_table': page_table,
        'w_qkv_a': nrm(ks[8], (N_A_LAYERS, D_MODEL, dqa), D_MODEL ** -0.5),
        'w_o_a': nrm(ks[9], (N_A_LAYERS, H_A * HEAD_DIM, D_MODEL), BETA * (H_A * HEAD_DIM) ** -0.5),
        'w_qkv_b': nrm(ks[10], (N_B_LAYERS, D_MODEL, dqb), D_MODEL ** -0.5),
        'w_o_b': nrm(ks[11], (N_B_LAYERS, H_B * HEAD_DIM, D_MODEL), BETA * (H_B * HEAD_DIM) ** -0.5),
        'sb_bias': SB_BIAS_INIT + nrm(ks[18], (N_B_LAYERS, H_B), 0.1),
        'w_ffn_in': nrm(ks[12], (DEPTH, D_MODEL, 2 * D_FF), D_MODEL ** -0.5),
        'w_ffn_out': nrm(ks[13], (DEPTH, D_FF, D_MODEL), BETA * D_FF ** -0.5),
        'ln_mix_g': 1.0 + nrm(ks[14], (DEPTH, D_MODEL), 0.02),
        'ln_mix_b': nrm(ks[15], (DEPTH, D_MODEL), 0.02),
        'ln_ffn_g': 1.0 + nrm(ks[16], (DEPTH, D_MODEL), 0.02),
        'ln_ffn_b': nrm(ks[17], (DEPTH, D_MODEL), 0.02),
    }


def reference(x_prompt, x_sample, cache_a_g0, cache_a_g1, cache_a_g2, cache_b_k, cache_b_v, page_table,
              w_qkv_a, w_o_a, w_qkv_b, w_o_b, sb_bias, w_ffn_in, w_ffn_out, ln_mix_g, ln_mix_b, ln_ffn_g, ln_ffn_b):
    xp, xs = x_prompt, x_sample
    a_prompt = [[] for _ in range(N_GROUPS)]
    a_sample = [[] for _ in range(N_GROUPS)]
    bk_p, bv_p, bk_s, bv_s = [], [], [], []
    for i in range(DEPTH):
        j = i // N_MIXERS
        if i % N_MIXERS == 0:
            yp, st_p = mixer_a_prompt(xp, w_qkv_a[j], w_o_a[j])
            ys, st_s = mixer_a_sample(xs, [cache_a_g0[j], cache_a_g1[j], cache_a_g2[j]], w_qkv_a[j], w_o_a[j])
            for g in range(N_GROUPS):
                a_prompt[g].append(st_p[g])
                a_sample[g].append(st_s[g])
        else:
            yp, kp_new, vp_new = mixer_b_prompt(xp, w_qkv_b[j], w_o_b[j], sb_bias[j])
            ys, ks_new, vs_new = mixer_b_sample(xs, cache_b_k[j][page_table], cache_b_v[j][page_table],
                                                w_qkv_b[j], w_o_b[j], sb_bias[j])
            bk_p.append(kp_new)
            bv_p.append(vp_new)
            bk_s.append(ks_new)
            bv_s.append(vs_new)
        xp = layer_norm(ALPHA * xp + yp, ln_mix_g[i], ln_mix_b[i])
        xs = layer_norm(ALPHA * xs + ys, ln_mix_g[i], ln_mix_b[i])
        xp = layer_norm(ALPHA * xp + swiglu(xp, w_ffn_in[i], w_ffn_out[i]), ln_ffn_g[i], ln_ffn_b[i])
        xs = layer_norm(ALPHA * xs + swiglu(xs, w_ffn_in[i], w_ffn_out[i]), ln_ffn_g[i], ln_ffn_b[i])
    new_a_g0_prompt = jnp.stack(a_prompt[0], axis=0)
    new_a_g0_sample = jnp.stack(a_sample[0], axis=0)
    new_a_g1_prompt = jnp.stack(a_prompt[1], axis=0)
    new_a_g1_sample = jnp.stack(a_sample[1], axis=0)
    new_a_g2_prompt = jnp.stack(a_prompt[2], axis=0)
    new_a_g2_sample = jnp.stack(a_sample[2], axis=0)
    new_b_k_prompt = jnp.stack(bk_p, axis=0)
    new_b_v_prompt = jnp.stack(bv_p, axis=0)
    new_b_k_sample = jnp.stack(bk_s, axis=0)
    new_b_v_sample = jnp.stack(bv_s, axis=0)
    return (xp, xs, new_a_g0_prompt, new_a_g0_sample, new_a_g1_prompt, new_a_g1_sample, new_a_g2_prompt,
            new_a_g2_sample, new_b_k_prompt, new_b_v_prompt, new_b_k_sample, new_b_v_sample)
```

```python
import functools
import math

import jax
import jax.numpy as jnp
from jax import lax
from jax.experimental import pallas as pl
from jax.experimental.pallas import tpu as pltpu

F32 = jnp.float32
BF16 = jnp.bfloat16

D_MODEL = 4096
DEPTH = 4
PAST_LEN = 8192
PAGE_SIZE = 128
HEAD_DIM = 128
DIL_GROUPS = ((128, 1), (512, 4), (2048, 16))
N_GROUPS = len(DIL_GROUPS)
H_A = 10
H_B = D_MODEL // HEAD_DIM
ROT_DIM = HEAD_DIM // 4
ROPE_THETA = 500000.0
D_FF = -(-8 * D_MODEL // (3 * 256)) * 256
QB = 128
ALPHA = (2.0 * DEPTH) ** 0.25
LN_EPS = 1e-5
SCALE = 1.0 / math.sqrt(HEAD_DIM)

LANES = 128
SAMPLE_ROWS = 16
D_FF_PAD = -(-D_FF // 512) * 512
VMEM_LIMIT = 56 * 1024 * 1024


def _cparams(sem):
    return pltpu.CompilerParams(dimension_semantics=sem, vmem_limit_bytes=VMEM_LIMIT)


def _dot(a, b):
    return jnp.dot(a, b, preferred_element_type=F32)


def _dot_nt(a, b):
    return lax.dot_general(a, b, (((1,), (1,)), ((), ())), preferred_element_type=F32)


def _proj_kernel(n_row_tiles, xp_ref, xs_ref, w_ref, yp_ref, ys_ref):
    i = pl.program_id(1)

    @pl.when(i < n_row_tiles)
    def _():
        yp_ref[...] = _dot(xp_ref[...], w_ref[...]).astype(yp_ref.dtype)

    @pl.when(i == n_row_tiles)
    def _():
        ys_ref[...] = _dot(xs_ref[...], w_ref[...]).astype(ys_ref.dtype)


def _rope_store(y, c, s1, s2, out_ref):
    for t in range(y.shape[1] // LANES):
        x = y[:, t * LANES:(t + 1) * LANES]
        r = x * c + pltpu.roll(x, ROT_DIM // 2, 1) * s1 + pltpu.roll(x, LANES - ROT_DIM // 2, 1) * s2
        out_ref[:, t * LANES:(t + 1) * LANES] = r


def _proj_rope_kernel(n_row_tiles, n_rope_tiles, xp_ref, xs_ref, w_ref, cp_ref, s1p_ref, s2p_ref,
                      cs_ref, s1s_ref, s2s_ref, yp_ref, ys_ref):
    j = pl.program_id(0)
    i = pl.program_id(1)
    rope = j < n_rope_tiles

    @pl.when((i < n_row_tiles) & rope)
    def _():
        _rope_store(_dot(xp_ref[...], w_ref[...]), cp_ref[...], s1p_ref[...], s2p_ref[...], yp_ref)

    @pl.when((i < n_row_tiles) & jnp.logical_not(rope))
    def _():
        yp_ref[...] = _dot(xp_ref[...], w_ref[...])

    @pl.when((i == n_row_tiles) & rope)
    def _():
        _rope_store(_dot(xs_ref[...], w_ref[...]), cs_ref[...], s1s_ref[...], s2s_ref[...], ys_ref)

    @pl.when((i == n_row_tiles) & jnp.logical_not(rope))
    def _():
        ys_ref[...] = _dot(xs_ref[...], w_ref[...])


def _swiglu_kernel(n_row_tiles, xp_ref, xs_ref, wg_ref, wu_ref, yp_ref, ys_ref):
    i = pl.program_id(1)

    def act(x):
        g = _dot(x, wg_ref[...])
        u = _dot(x, wu_ref[...])
        return ((g * (1.0 / (1.0 + jnp.exp(-g)))) * u).astype(yp_ref.dtype)

    @pl.when(i < n_row_tiles)
    def _():
        yp_ref[...] = act(xp_ref[...])

    @pl.when(i == n_row_tiles)
    def _():
        ys_ref[...] = act(xs_ref[...])


def _row_idx(n_row_tiles):
    return lambda j, i: (jnp.minimum(i, n_row_tiles - 1), 0)


def proj(xp, xs, w, *, col_off=0, n_cols=None, bm, bn, out_dtype, name):
    mp, kd = xp.shape
    n_cols = w.shape[1] if n_cols is None else n_cols
    nI, nJ = mp // bm, n_cols // bn
    assert nI * bm == mp and nJ * bn == n_cols and col_off % bn == 0
    joff = col_off // bn
    ms = xs.shape[0]
    return pl.pallas_call(
        functools.partial(_proj_kernel, nI),
        grid=(nJ, nI + 1),
        in_specs=[
            pl.BlockSpec((bm, kd), _row_idx(nI)),
            pl.BlockSpec((ms, kd), lambda j, i: (0, 0)),
            pl.BlockSpec((kd, bn), lambda j, i: (0, j + joff)),
        ],
        out_specs=[
            pl.BlockSpec((bm, bn), lambda j, i: (jnp.minimum(i, nI - 1), j)),
            pl.BlockSpec((ms, bn), lambda j, i: (0, j)),
        ],
        out_shape=[jax.ShapeDtypeStruct((mp, n_cols), out_dtype),
                   jax.ShapeDtypeStruct((ms, n_cols), out_dtype)],
        compiler_params=_cparams(("arbitrary", "arbitrary")),
        name=name,
    )(xp, xs, w)


def proj_rope(xp, xs, w, tabs_p, tabs_s, *, n_rope_cols, bm, bn, name):
    mp, kd = xp.shape
    n_cols = w.shape[1]
    nI, nJ = mp // bm, n_cols // bn
    assert nI * bm == mp and nJ * bn == n_cols and n_rope_cols % bn == 0 and bn % LANES == 0
    ms = xs.shape[0]
    tab_p = pl.BlockSpec((bm, LANES), _row_idx(nI))
    tab_s = pl.BlockSpec((ms, LANES), lambda j, i: (0, 0))
    return pl.pallas_call(
        functools.partial(_proj_rope_kernel, nI, n_rope_cols // bn),
        grid=(nJ, nI + 1),
        in_specs=[
            pl.BlockSpec((bm, kd), _row_idx(nI)),
            pl.BlockSpec((ms, kd), lambda j, i: (0, 0)),
            pl.BlockSpec((kd, bn), lambda j, i: (0, j)),
            tab_p, tab_p, tab_p, tab_s, tab_s, tab_s,
        ],
        out_specs=[
            pl.BlockSpec((bm, bn), lambda j, i: (jnp.minimum(i, nI - 1), j)),
            pl.BlockSpec((ms, bn), lambda j, i: (0, j)),
        ],
        out_shape=[jax.ShapeDtypeStruct((mp, n_cols), F32),
                   jax.ShapeDtypeStruct((ms, n_cols), F32)],
        compiler_params=_cparams(("arbitrary", "arbitrary")),
        name=name,
    )(xp, xs, w, *tabs_p, *tabs_s)


def swiglu_in(xp, xs, wg, wu, *, bm, bn, name):
    mp, kd = xp.shape
    n_cols = wg.shape[1]
    nI, nJ = mp // bm, n_cols // bn
    assert nI * bm == mp and nJ * bn == n_cols
    ms = xs.shape[0]
    wspec = pl.BlockSpec((kd, bn), lambda j, i: (0, j))
    return pl.pallas_call(
        functools.partial(_swiglu_kernel, nI),
        grid=(nJ, nI + 1),
        in_specs=[
            pl.BlockSpec((bm, kd), _row_idx(nI)),
            pl.BlockSpec((ms, kd), lambda j, i: (0, 0)),
            wspec, wspec,
        ],
        out_specs=[
            pl.BlockSpec((bm, bn), lambda j, i: (jnp.minimum(i, nI - 1), j)),
            pl.BlockSpec((ms, bn), lambda j, i: (0, j)),
        ],
        out_shape=[jax.ShapeDtypeStruct((mp, n_cols), BF16),
                   jax.ShapeDtypeStruct((ms, n_cols), BF16)],
        compiler_params=_cparams(("arbitrary", "arbitrary")),
        name=name,
    )(xp, xs, wg, wu)


LN_CHUNK = 16
LN_COLS = 1024


def _ln_rows(acc_ref, r_ref, g_ref, b_ref, of_ref, ob_ref):
    rows = acc_ref.shape[0]
    g = g_ref[...]
    b = b_ref[...]

    def body(c, carry):
        sl = pl.ds(pl.multiple_of(c * LN_CHUNK, LN_CHUNK), LN_CHUNK)
        y = ALPHA * r_ref[sl, :] + acc_ref[sl, :]
        mu = jnp.mean(y, axis=-1, keepdims=True)
        yc = y - mu
        var = jnp.mean(yc * yc, axis=-1, keepdims=True)
        out = yc * lax.rsqrt(var + LN_EPS) * g + b
        of_ref[sl, :] = out
        ob_ref[sl, :] = out.astype(BF16)
        return carry

    lax.fori_loop(0, rows // LN_CHUNK, body, 0)


def _mm_ln_kernel(n_row_tiles, n_k, ap_ref, as_ref, w_ref, rp_ref, rs_ref, g_ref, b_ref,
                  opf_ref, opb_ref, osf_ref, osb_ref):
    i = pl.program_id(0)
    k = pl.program_id(1)

    def run(a_ref, r_ref, of_ref, ob_ref):
        cols = min(LN_COLS, of_ref.shape[1])
        for c in range(of_ref.shape[1] // cols):
            cs = slice(c * cols, (c + 1) * cols)
            d = _dot(a_ref[...], w_ref[:, cs])
            if n_k == 1:
                of_ref[:, cs] = d
            else:
                @pl.when(k == 0)
                def _():
                    of_ref[:, cs] = d

                @pl.when(k > 0)
                def _():
                    of_ref[:, cs] += d

        @pl.when(k == n_k - 1)
        def _():
            _ln_rows(of_ref, r_ref, g_ref, b_ref, of_ref, ob_ref)

    @pl.when(i < n_row_tiles)
    def _():
        run(ap_ref, rp_ref, opf_ref, opb_ref)

    @pl.when(i == n_row_tiles)
    def _():
        run(as_ref, rs_ref, osf_ref, osb_ref)


def mm_ln(ap, as_, w, rp, rs, g, b, *, bm, bk, name):
    mp, kd = ap.shape
    dm = w.shape[1]
    nI, nK = mp // bm, kd // bk
    assert nI * bm == mp and nK * bk == kd
    ms = as_.shape[0]
    rowp = lambda i, k: (jnp.minimum(i, nI - 1), 0)
    rows = lambda i, k: (0, 0)
    return pl.pallas_call(
        functools.partial(_mm_ln_kernel, nI, nK),
        grid=(nI + 1, nK),
        in_specs=[
            pl.BlockSpec((bm, bk), lambda i, k: (jnp.minimum(i, nI - 1), k)),
            pl.BlockSpec((ms, bk), lambda i, k: (0, k)),
            pl.BlockSpec((bk, dm), lambda i, k: (k, 0)),
            pl.BlockSpec((bm, dm), rowp),
            pl.BlockSpec((ms, dm), rows),
            pl.BlockSpec((1, dm), rows),
            pl.BlockSpec((1, dm), rows),
        ],
        out_specs=[
            pl.BlockSpec((bm, dm), rowp),
            pl.BlockSpec((bm, dm), rowp),
            pl.BlockSpec((ms, dm), rows),
            pl.BlockSpec((ms, dm), rows),
        ],
        out_shape=[jax.ShapeDtypeStruct((mp, dm), F32), jax.ShapeDtypeStruct((mp, dm), BF16),
                   jax.ShapeDtypeStruct((ms, dm), F32), jax.ShapeDtypeStruct((ms, dm), BF16)],
        compiler_params=_cparams(("arbitrary", "arbitrary")),
        name=name,
    )(ap, as_, w, rp, rs, g.reshape(1, dm), b.reshape(1, dm))


def _softmax_block(q, k, v, mask):
    s = _dot_nt(q.astype(BF16), k.astype(BF16)) * SCALE
    s = jnp.where(mask, s, -jnp.inf)
    m = jnp.max(s, axis=-1, keepdims=True)
    p = jnp.exp(s - m)
    l = jnp.sum(p, axis=-1, keepdims=True)
    o = _dot(p.astype(BF16), v.astype(BF16)) / l
    return o, m + jnp.log(l)


def _attn_a_prompt_kernel(seq, q0, q1, q2, k0, k1, k2, v0, v1, v2, o_ref, os0, os1, os2, ls0, ls1, ls2):
    qs, ks, vs = (q0, q1, q2), (k0, k1, k2), (v0, v1, v2)
    o_scr, l_scr = (os0, os1, os2), (ls0, ls1, ls2)
    qq = lax.broadcasted_iota(jnp.int32, (QB, QB), 0)
    kk = lax.broadcasted_iota(jnp.int32, (QB, QB), 1)
    causal = kk <= qq
    qq2 = lax.broadcasted_iota(jnp.int32, (QB, 2 * QB), 0)
    kk2 = lax.broadcasted_iota(jnp.int32, (QB, 2 * QB), 1)
    band = (kk2 >= qq2) & (kk2 <= qq2 + QB)

    def rows(start, n, d):
        return pl.ds(start, n) if d == 1 else pl.ds(start, n, stride=d)

    def emit(g, q_rows, k_rows, mask):
        o, lse = _softmax_block(qs[g][q_rows, :], ks[g][k_rows, :], vs[g][k_rows, :], mask)
        o_scr[g][q_rows, :] = o
        l_scr[g][q_rows, :] = jnp.broadcast_to(lse, (QB, LANES))

    for g, (window, d) in enumerate(DIL_GROUPS):
        assert window // d == QB
        sub_len = seq // d
        n_blk = sub_len // QB
        for r in range(d):
            emit(g, rows(r, QB, d), rows(r, QB, d), causal)
            if n_blk > 1:
                def body(c, carry, g=g, r=r, d=d):
                    start = pl.multiple_of(c * (QB * d), QB * d)
                    emit(g, rows(start + r, QB, d), rows(start - QB * d + r, 2 * QB, d), band)
                    return carry
                lax.fori_loop(1, n_blk, body, 0)

    def merge(c, carry):
        sl = pl.ds(pl.multiple_of(c * QB, QB), QB)
        l0, l1, l2 = ls0[sl, :], ls1[sl, :], ls2[sl, :]
        mx = jnp.maximum(jnp.maximum(l0, l1), l2)
        w0, w1, w2 = jnp.exp(l0 - mx), jnp.exp(l1 - mx), jnp.exp(l2 - mx)
        den = w0 + w1 + w2
        out = (w0 / den) * os0[sl, :] + (w1 / den) * os1[sl, :] + (w2 / den) * os2[sl, :]
        o_ref[sl, :] = out.astype(o_ref.dtype)
        return carry

    lax.fori_loop(0, seq // QB, merge, 0)


def attn_a_prompt(qkv, batch, seq):
    def spec(sec, g):
        base = (sec * N_GROUPS + g) * H_A
        return pl.BlockSpec((seq, HEAD_DIM), lambda b, h: (b, base + h))
    in_specs = [spec(sec, g) for sec in range(3) for g in range(N_GROUPS)]
    return pl.pallas_call(
        functools.partial(_attn_a_prompt_kernel, seq),
        grid=(batch, H_A),
        in_specs=in_specs,
        out_specs=pl.BlockSpec((seq, HEAD_DIM), lambda b, h: (b, h)),
        out_shape=jax.ShapeDtypeStruct((batch * seq, H_A * HEAD_DIM), BF16),
        scratch_shapes=[pltpu.VMEM((seq, HEAD_DIM), F32)] * (2 * N_GROUPS),
        compiler_params=_cparams(("arbitrary", "arbitrary")),
        name="attn_a_prompt",
    )(*([qkv] * 9))


def _attn_a_sample_kernel(qkv_ref, c0_ref, c1_ref, c2_ref, o_ref):
    b = pl.program_id(0)
    caches = (c0_ref, c1_ref, c2_ref)
    row = qkv_ref[pl.ds(b, 1), :]
    hw = H_A * HEAD_DIM

    def piece(sec, g, h):
        base = (sec * N_GROUPS + g) * hw + h * HEAD_DIM
        return row[:, base:base + HEAD_DIM]

    for h in range(H_A):
        outs, lses = [], []
        for g in range(N_GROUPS):
            q, kn, vn = piece(0, g, h), piece(1, g, h), piece(2, g, h)
            kc = caches[g][0, :, h * HEAD_DIM:(h + 1) * HEAD_DIM]
            vc = caches[g][0, :, hw + h * HEAD_DIM:hw + (h + 1) * HEAD_DIM]
            sc = jnp.sum(kc * q, axis=-1, keepdims=True) * SCALE
            sn = jnp.sum(kn * q, axis=-1, keepdims=True) * SCALE
            m = jnp.maximum(jnp.max(sc, axis=0, keepdims=True), sn)
            pc = jnp.exp(sc - m)
            pn = jnp.exp(sn - m)
            l = jnp.sum(pc, axis=0, keepdims=True) + pn
            o = (jnp.sum(pc * vc, axis=0, keepdims=True) + pn * vn) / l
            outs.append(o)
            lses.append(m + jnp.log(l))
        mx = jnp.maximum(jnp.maximum(lses[0], lses[1]), lses[2])
        ws = [jnp.exp(l - mx) for l in lses]
        den = ws[0] + ws[1] + ws[2]
        out = (ws[0] / den) * outs[0] + (ws[1] / den) * outs[1] + (ws[2] / den) * outs[2]
        o_ref[0, :, h * HEAD_DIM:(h + 1) * HEAD_DIM] = out


def attn_a_sample(qkv_s, caches):
    nb = caches[0].shape[0]
    hw2 = 2 * H_A * HEAD_DIM
    views, specs = [], []
    for g, (window, d) in enumerate(DIL_GROUPS):
        lbuf = caches[g].shape[1]
        assert lbuf == QB * d
        views.append(caches[g][:, ::d].reshape(nb, QB, hw2))
        specs.append(pl.BlockSpec((1, QB, hw2), lambda b: (b, 0, 0)))
    return pl.pallas_call(
        _attn_a_sample_kernel,
        grid=(nb,),
        in_specs=[pl.BlockSpec(qkv_s.shape, lambda b: (0, 0))] + specs,
        out_specs=pl.BlockSpec((1, 1, H_A * HEAD_DIM), lambda b: (b, 0, 0)),
        out_shape=jax.ShapeDtypeStruct((nb, 1, H_A * HEAD_DIM), F32),
        compiler_params=_cparams(("arbitrary",)),
        name="attn_a_sample",
    )(qkv_s, *views)


def _log_sigmoids(z):
    ls = jnp.minimum(z, 0.0) - jnp.log(1.0 + jnp.exp(-jnp.abs(z)))
    return ls, ls - z


def _split_bf16(x):
    hi = x.astype(BF16)
    lo = (x - hi.astype(F32)).astype(BF16)
    return hi, lo


SB_TQ = 256
SB_TK = 128


def _attn_b_prompt_kernel(seq, bias_ref, q_ref, k_ref, v_ref, o_ref, qb_scr, kb_scr, vt_scr):
    h = pl.program_id(1)
    bias = bias_ref[h]
    qb_scr[...] = q_ref[...].astype(BF16)
    kb_scr[...] = k_ref[...].astype(BF16)
    for c in range(seq // SB_TK):
        vt_scr[c] = v_ref[c * SB_TK:(c + 1) * SB_TK, :].T.astype(BF16)

    si = lax.broadcasted_iota(jnp.int32, (SB_TK, 2 * SB_TK), 0)
    ji = lax.broadcasted_iota(jnp.int32, (SB_TK, 2 * SB_TK), 1)
    upper2 = jnp.where((ji % SB_TK) > si, 1.0, 0.0).astype(BF16)

    kpos = lax.broadcasted_iota(jnp.int32, (SB_TK, SB_TQ), 0)
    qpos = lax.broadcasted_iota(jnp.int32, (SB_TK, SB_TQ), 1)

    def block(qi, k_start, valid, carry, acc):
        qb = qb_scr[pl.ds(qi * SB_TQ, SB_TQ), :]
        kb = kb_scr[pl.ds(k_start, SB_TK), :]
        z = _dot_nt(kb, qb) * SCALE + bias
        ls, lk = _log_sigmoids(z)
        if valid is not None:
            lk = jnp.where(valid, lk, 0.0)
        hi, lo = _split_bf16(lk)
        after = _dot(upper2, jnp.concatenate([hi, lo], axis=0))
        a = jnp.exp(ls + after + carry)
        if valid is not None:
            a = jnp.where(valid, a, 0.0)
        acc = acc + _dot(vt_scr[k_start // SB_TK], a.astype(BF16))
        carry = carry + after[0:1, :] + lk[0:1, :]
        return carry, acc

    for qi in range(seq // SB_TQ):
        carry = jnp.zeros((1, SB_TQ), F32)
        acc = jnp.zeros((HEAD_DIM, SB_TQ), F32)
        n_full = qi * (SB_TQ // SB_TK)
        for d in reversed(range(SB_TQ // SB_TK)):
            off = d * SB_TK
            valid = (kpos + off) < qpos
            carry, acc = block(qi, qi * SB_TQ + off, valid, carry, acc)
        if n_full > 0:
            def body(t, ca, qi=qi, n_full=n_full):
                k_start = pl.multiple_of((n_full - 1 - t) * SB_TK, SB_TK)
                return block(qi, k_start, None, ca[0], ca[1])
            carry, acc = lax.fori_loop(0, n_full, body, (carry, acc))
        o_ref[qi * SB_TQ:(qi + 1) * SB_TQ, :] = acc.T.astype(o_ref.dtype)


def attn_b_prompt(q, k, v, bias, batch, seq):
    n_heads = q.shape[1] // HEAD_DIM
    spec = pl.BlockSpec((seq, HEAD_DIM), lambda b, h: (b, h))
    return pl.pallas_call(
        functools.partial(_attn_b_prompt_kernel, seq),
        grid=(batch, n_heads),
        in_specs=[pl.BlockSpec(memory_space=pltpu.SMEM), spec, spec, spec],
        out_specs=spec,
        scratch_shapes=[pltpu.VMEM((seq, HEAD_DIM), BF16), pltpu.VMEM((seq, HEAD_DIM), BF16),
                        pltpu.VMEM((seq // SB_TK, HEAD_DIM, SB_TK), BF16)],
        out_shape=jax.ShapeDtypeStruct((batch * seq, n_heads * HEAD_DIM), BF16),
        compiler_params=_cparams(("arbitrary", "arbitrary")),
        name="attn_b_prompt",
    )(bias, q, k, v)


def _attn_b_sample_kernel(n_pages, n_heads, pt_ref, q_ref, bias_ref, k_ref, v_ref, o_ref,
                          acc_scr, carry_scr):
    b = pl.program_id(0)
    p = pl.program_id(1)
    page = PAGE_SIZE

    @pl.when(p == 0)
    def _():
        acc_scr[...] = jnp.zeros_like(acc_scr)
        carry_scr[...] = jnp.zeros_like(carry_scr)

    qrow = q_ref[pl.ds(b, 1), :]
    lane = lax.broadcasted_iota(jnp.int32, (page, LANES), 1)
    z = jnp.zeros((page, LANES), F32)
    for h in range(n_heads):
        kh = k_ref[pl.ds(h, page, stride=n_heads), :]
        zh = jnp.sum(kh * qrow[:, h * HEAD_DIM:(h + 1) * HEAD_DIM], axis=-1, keepdims=True)
        z = jnp.where(lane == h, zh, z)
    z = z * SCALE + bias_ref[...]
    ls, lk = _log_sigmoids(z)
    si = lax.broadcasted_iota(jnp.int32, (page, 2 * page), 0)
    ji = lax.broadcasted_iota(jnp.int32, (page, 2 * page), 1)
    upper2 = jnp.where((ji % page) > si, 1.0, 0.0).astype(BF16)
    hi, lo = _split_bf16(lk)
    after = _dot(upper2, jnp.concatenate([hi, lo], axis=0))
    a = jnp.exp(ls + after + carry_scr[...])
    carry_scr[...] += after[0:1, :] + lk[0:1, :]
    for h in range(n_heads):
        vh = v_ref[pl.ds(h, page, stride=n_heads), :]
        contrib = jnp.broadcast_to(a[:, h:h + 1], (page, HEAD_DIM)) * vh
        acc_scr[h] += jnp.sum(contrib.reshape(page // 8, 8, HEAD_DIM), axis=0)

    @pl.when(p == n_pages - 1)
    def _():
        for h in range(n_heads):
            o_ref[0, :, h * HEAD_DIM:(h + 1) * HEAD_DIM] = jnp.sum(acc_scr[h], axis=0, keepdims=True)


def attn_b_sample(q_s, bias, cache_k, cache_v, page_table, layer):
    nb, n_pages = page_table.shape
    n_layers, n_pool, page, n_heads, dh = cache_k.shape
    dm = n_heads * dh
    assert page == PAGE_SIZE and dh == HEAD_DIM and n_heads <= LANES and n_heads % 8 == 0
    ck = cache_k.reshape(n_layers, n_pool, page * n_heads, dh)
    cv = cache_v.reshape(n_layers, n_pool, page * n_heads, dh)
    page_spec = pl.BlockSpec((None, None, page * n_heads, dh),
                             lambda b, p, pt: (layer, pt[b * n_pages + (n_pages - 1 - p)], 0, 0))
    bias_row = jnp.pad(bias, (0, LANES - n_heads)).reshape(1, LANES)
    grid_spec = pltpu.PrefetchScalarGridSpec(
        num_scalar_prefetch=1,
        grid=(nb, n_pages),
        in_specs=[pl.BlockSpec(q_s.shape, lambda b, p, pt: (0, 0)),
                  pl.BlockSpec((1, LANES), lambda b, p, pt: (0, 0)),
                  page_spec, page_spec],
        out_specs=pl.BlockSpec((1, 1, dm), lambda b, p, pt: (b, 0, 0)),
        scratch_shapes=[pltpu.VMEM((n_heads, 8, dh), F32), pltpu.VMEM((1, LANES), F32)],
    )
    return pl.pallas_call(
        functools.partial(_attn_b_sample_kernel, n_pages, n_heads),
        grid_spec=grid_spec,
        out_shape=jax.ShapeDtypeStruct((nb, 1, dm), F32),
        compiler_params=_cparams(("arbitrary", "arbitrary")),
        name="attn_b_sample",
    )(page_table.reshape(-1), q_s, bias_row, ck, cv)


def _rope_tables(pos):
    half = ROT_DIM // 2
    inv_freq = ROPE_THETA ** (-jnp.arange(half, dtype=F32) / half)
    ang = pos.astype(F32)[:, None] * inv_freq[None, :]
    cos, sin = jnp.cos(ang), jnp.sin(ang)
    n = pos.shape[0]
    rest = HEAD_DIM - ROT_DIM
    c = jnp.concatenate([cos, cos, jnp.ones((n, rest), F32)], axis=1)
    s1 = jnp.concatenate([jnp.zeros((n, half), F32), sin, jnp.zeros((n, rest), F32)], axis=1)
    s2 = jnp.concatenate([-sin, jnp.zeros((n, half + rest), F32)], axis=1)
    return c, s1, s2


def _pad_rows(x, rows):
    return jnp.pad(x, ((0, rows - x.shape[0]), (0, 0)))


def kernel(x_prompt, x_sample, cache_a_g0, cache_a_g1, cache_a_g2, cache_b_k, cache_b_v, page_table, w_qkv_a, w_o_a, w_qkv_b, w_o_b, sb_bias, w_ffn_in, w_ffn_out, ln_mix_g, ln_mix_b, ln_ffn_g, ln_ffn_b):
    batch, seq, dm = x_prompt.shape
    nb, dec_seq, _ = x_sample.shape
    assert dec_seq == 1 and nb <= SAMPLE_ROWS
    mp = batch * seq
    caches_a = (cache_a_g0, cache_a_g1, cache_a_g2)
    hw = H_A * HEAD_DIM

    wqa = w_qkv_a.astype(BF16)
    woa = w_o_a.astype(BF16)
    wqb = w_qkv_b.astype(BF16)
    wob = w_o_b.astype(BF16)
    ff_pad = D_FF_PAD - D_FF
    wg = jnp.pad(w_ffn_in[:, :, :D_FF].astype(BF16), ((0, 0), (0, 0), (0, ff_pad)))
    wu = jnp.pad(w_ffn_in[:, :, D_FF:].astype(BF16), ((0, 0), (0, 0), (0, ff_pad)))
    wout = jnp.pad(w_ffn_out.astype(BF16), ((0, 0), (0, ff_pad), (0, 0)))

    xp = x_prompt.reshape(mp, dm)
    xs = _pad_rows(x_sample.reshape(nb, dm), SAMPLE_ROWS)
    xp_b, xs_b = xp.astype(BF16), xs.astype(BF16)

    tabs_p = _rope_tables(jnp.tile(jnp.arange(seq), batch))
    tabs_s = _rope_tables(jnp.full((SAMPLE_ROWS,), PAST_LEN, jnp.int32))

    a_prompt = [[] for _ in range(N_GROUPS)]
    a_sample = [[] for _ in range(N_GROUPS)]
    bk_p, bv_p, bk_s, bv_s = [], [], [], []

    for i in range(DEPTH):
        j = i // 2
        if i % 2 == 0:
            qkv_p, qkv_s = proj_rope(xp_b, xs_b, wqa[j], tabs_p, tabs_s, n_rope_cols=2 * N_GROUPS * hw,
                                     bm=1024, bn=768, name="qkv_a")
            att_p = attn_a_prompt(qkv_p, batch, seq)
            att_s = attn_a_sample(qkv_s, [c[j] for c in caches_a])
            att_s = _pad_rows(att_s.reshape(nb, hw), SAMPLE_ROWS).astype(BF16)
            for g, (window, _) in enumerate(DIL_GROUPS):
                keep = min(window, seq)
                kcol, vcol = (N_GROUPS + g) * hw, (2 * N_GROUPS + g) * hw
                q3 = qkv_p.reshape(batch, seq, -1)
                kp = q3[:, seq - keep:, kcol:kcol + hw].reshape(batch, keep, H_A, HEAD_DIM)
                vp = q3[:, seq - keep:, vcol:vcol + hw].reshape(batch, keep, H_A, HEAD_DIM)
                a_prompt[g].append(jnp.stack([kp, vp], axis=2))
                kn = qkv_s[:nb, kcol:kcol + hw].reshape(nb, 1, 1, H_A, HEAD_DIM)
                vn = qkv_s[:nb, vcol:vcol + hw].reshape(nb, 1, 1, H_A, HEAD_DIM)
                new = jnp.concatenate([kn, vn], axis=2)
                a_sample[g].append(jnp.concatenate([caches_a[g][j][:, 1:], new], axis=1))
            xp, xp_b, xs, xs_b = mm_ln(att_p, att_s, woa[j], xp, xs, ln_mix_g[i], ln_mix_b[i],
                                       bm=512, bk=256, name="o_a_ln")
        else:
            outs = [proj(xp_b, xs_b, wqb[j], col_off=sec * dm, n_cols=dm, bm=1024, bn=1024,
                         out_dtype=F32, name="qkv_b") for sec in range(3)]
            (q_p, q_s), (k_p, k_s), (v_p, v_s) = outs
            att_p = attn_b_prompt(q_p, k_p, v_p, sb_bias[j], batch, seq)
            att_s = attn_b_sample(q_s, sb_bias[j], cache_b_k, cache_b_v, page_table, j)
            att_s = _pad_rows(att_s.reshape(nb, dm), SAMPLE_ROWS).astype(BF16)
            bk_p.append(k_p.reshape(batch, seq, H_B, HEAD_DIM))
            bv_p.append(v_p.reshape(batch, seq, H_B, HEAD_DIM))
            bk_s.append(k_s[:nb].reshape(nb, 1, H_B, HEAD_DIM))
            bv_s.append(v_s[:nb].reshape(nb, 1, H_B, HEAD_DIM))
            xp, xp_b, xs, xs_b = mm_ln(att_p, att_s, wob[j], xp, xs, ln_mix_g[i], ln_mix_b[i],
                                       bm=512, bk=512, name="o_b_ln")
        act_p, act_s = swiglu_in(xp_b, xs_b, wg[i], wu[i], bm=1024, bn=512, name="ffn_in")
        xp, xp_b, xs, xs_b = mm_ln(act_p, act_s, wout[i], xp, xs, ln_ffn_g[i], ln_ffn_b[i],
                                   bm=512, bk=512, name="ffn_out_ln")

    y_prompt = xp.reshape(batch, seq, dm)
    y_sample = xs[:nb].reshape(nb, 1, dm)
    st = lambda parts: jnp.stack(parts, axis=0)
    return (y_prompt, y_sample,
            st(a_prompt[0]), st(a_sample[0]), st(a_prompt[1]), st(a_sample[1]), st(a_prompt[2]), st(a_sample[2]),
            st(bk_p), st(bv_p), st(bk_s), st(bv_s))
```

```python
import functools
import math

import jax
import jax.numpy as jnp
from jax import lax
from jax.experimental import pallas as pl
from jax.experimental.pallas import tpu as pltpu

F32 = jnp.float32
BF16 = jnp.bfloat16

D_MODEL = 4096
DEPTH = 4
PAST_LEN = 8192
PAGE_SIZE = 128
HEAD_DIM = 128
DIL_GROUPS = ((128, 1), (512, 4), (2048, 16))
N_GROUPS = len(DIL_GROUPS)
H_A = 10
H_B = D_MODEL // HEAD_DIM
ROT_DIM = HEAD_DIM // 4
ROPE_THETA = 500000.0
D_FF = -(-8 * D_MODEL // (3 * 256)) * 256
QB = 128
ALPHA = (2.0 * DEPTH) ** 0.25
LN_EPS = 1e-5
SCALE = 1.0 / math.sqrt(HEAD_DIM)

LANES = 128
SUBLANES = 8
SAMPLE_ROWS = 16
VMEM_LIMIT = 56 * 1024 * 1024


def _cparams(sem):
    return pltpu.CompilerParams(dimension_semantics=sem, vmem_limit_bytes=VMEM_LIMIT)


def _dot(a, b):
    return jnp.dot(a, b, preferred_element_type=F32)


def _dot_nt(a, b):
    return lax.dot_general(a, b, (((1,), (1,)), ((), ())), preferred_element_type=F32)


def _rope(x, c, s1, s2):
    return x * c + pltpu.roll(x, ROT_DIM // 2, 1) * s1 + pltpu.roll(x, LANES - ROT_DIM // 2, 1) * s2


def _proj_kernel(n_row_tiles, n_rope_tiles, has_resid, n_out, *refs):
    xp_ref, xs_ref, w_ref = refs[:3]
    pos = 3
    if has_resid:
        rp_ref, rs_ref = refs[pos:pos + 2]
        pos += 2
    if n_rope_tiles:
        tabs_p, tabs_s = refs[pos:pos + 3], refs[pos + 3:pos + 6]
        pos += 6
    outs_p, outs_s = refs[pos:pos + n_out], refs[pos + n_out:pos + 2 * n_out]
    j = pl.program_id(0)
    i = pl.program_id(1)

    def emit(x_ref, r_ref, tabs, outs):
        y = _dot(x_ref[...], w_ref[...])
        if has_resid:
            y = ALPHA * r_ref[...] + y

        def plain():
            for o in outs:
                o[...] = y.astype(o.dtype)

        if not n_rope_tiles:
            plain()
            return

        @pl.when(j < n_rope_tiles)
        def _():
            c, s1, s2 = tabs[0][...], tabs[1][...], tabs[2][...]
            for t in range(y.shape[1] // LANES):
                r = _rope(y[:, t * LANES:(t + 1) * LANES], c, s1, s2)
                for o in outs:
                    o[:, t * LANES:(t + 1) * LANES] = r.astype(o.dtype)

        pl.when(j >= n_rope_tiles)(plain)

    @pl.when(i < n_row_tiles)
    def _():
        emit(xp_ref, rp_ref if has_resid else None, tabs_p if n_rope_tiles else None, outs_p)

    @pl.when(i == n_row_tiles)
    def _():
        emit(xs_ref, rs_ref if has_resid else None, tabs_s if n_rope_tiles else None, outs_s)


def proj(xp, xs, w, *, bm, bn, out_dtypes, name, col_off=0, n_cols=None, resid=None, rope=None):
    mp, kd = xp.shape
    ms = xs.shape[0]
    n_cols = w.shape[1] if n_cols is None else n_cols
    nI, nJ = mp // bm, n_cols // bn
    assert nI * bm == mp and nJ * bn == n_cols and col_off % bn == 0 and bn % LANES == 0
    joff = col_off // bn
    row_p = lambda j, i: (jnp.minimum(i, nI - 1), 0)
    tile_p = lambda j, i: (jnp.minimum(i, nI - 1), j)
    fixed = lambda j, i: (0, 0)
    tile_s = lambda j, i: (0, j)
    args = [xp, xs, w]
    in_specs = [pl.BlockSpec((bm, kd), row_p), pl.BlockSpec((ms, kd), fixed),
                pl.BlockSpec((kd, bn), lambda j, i: (0, j + joff))]
    if resid is not None:
        args += list(resid)
        in_specs += [pl.BlockSpec((bm, bn), tile_p), pl.BlockSpec((ms, bn), tile_s)]
    n_rope_tiles = 0
    if rope is not None:
        tabs_p, tabs_s, n_rope_cols = rope
        assert n_rope_cols % bn == 0
        n_rope_tiles = n_rope_cols // bn
        args += list(tabs_p) + list(tabs_s)
        in_specs += [pl.BlockSpec((bm, LANES), row_p)] * 3 + [pl.BlockSpec((ms, LANES), fixed)] * 3
    n_out = len(out_dtypes)
    outs = pl.pallas_call(
        functools.partial(_proj_kernel, nI, n_rope_tiles, resid is not None, n_out),
        grid=(nJ, nI + 1),
        in_specs=in_specs,
        out_specs=[pl.BlockSpec((bm, bn), tile_p)] * n_out + [pl.BlockSpec((ms, bn), tile_s)] * n_out,
        out_shape=[jax.ShapeDtypeStruct((mp, n_cols), dt) for dt in out_dtypes]
                  + [jax.ShapeDtypeStruct((ms, n_cols), dt) for dt in out_dtypes],
        compiler_params=_cparams(("arbitrary", "arbitrary")),
        name=name,
    )(*args)
    return outs[:n_out], outs[n_out:]


W_CAST_ROWS = 512


def _swiglu_kernel(n_row_tiles, xp_ref, xs_ref, wg_ref, wu_ref, yp_ref, ys_ref, wgb_scr, wub_scr):
    i = pl.program_id(1)

    @pl.when(i == 0)
    def _():
        for r in range(0, wg_ref.shape[0], W_CAST_ROWS):
            wgb_scr[r:r + W_CAST_ROWS, :] = wg_ref[r:r + W_CAST_ROWS, :].astype(BF16)
            wub_scr[r:r + W_CAST_ROWS, :] = wu_ref[r:r + W_CAST_ROWS, :].astype(BF16)

    def act(x):
        g = _dot(x, wgb_scr[...])
        u = _dot(x, wub_scr[...])
        return ((g * (1.0 / (1.0 + jnp.exp(-g)))) * u).astype(yp_ref.dtype)

    @pl.when(i < n_row_tiles)
    def _():
        yp_ref[...] = act(xp_ref[...])

    @pl.when(i == n_row_tiles)
    def _():
        ys_ref[...] = act(xs_ref[...])


def swiglu_in(xp, xs, w_in, layer, *, bm, bn, name):
    mp, kd = xp.shape
    ms = xs.shape[0]
    ff = w_in.shape[2] // 2
    nI, nJ = mp // bm, ff // bn
    assert nI * bm == mp and nJ * bn == ff and kd % W_CAST_ROWS == 0
    return pl.pallas_call(
        functools.partial(_swiglu_kernel, nI),
        grid=(nJ, nI + 1),
        in_specs=[
            pl.BlockSpec((bm, kd), lambda j, i: (jnp.minimum(i, nI - 1), 0)),
            pl.BlockSpec((ms, kd), lambda j, i: (0, 0)),
            pl.BlockSpec((None, kd, bn), lambda j, i: (layer, 0, j)),
            pl.BlockSpec((None, kd, bn), lambda j, i: (layer, 0, j + nJ)),
        ],
        out_specs=[
            pl.BlockSpec((bm, bn), lambda j, i: (jnp.minimum(i, nI - 1), j)),
            pl.BlockSpec((ms, bn), lambda j, i: (0, j)),
        ],
        out_shape=[jax.ShapeDtypeStruct((mp, ff), BF16), jax.ShapeDtypeStruct((ms, ff), BF16)],
        scratch_shapes=[pltpu.VMEM((kd, bn), BF16), pltpu.VMEM((kd, bn), BF16)],
        compiler_params=_cparams(("arbitrary", "arbitrary")),
        name=name,
    )(xp, xs, w_in, w_in)


LN_CHUNK = 16


def _ln_kernel(y_ref, g_ref, b_ref, of_ref, ob_ref):
    g = g_ref[...]
    b = b_ref[...]

    def body(c, carry):
        sl = pl.ds(pl.multiple_of(c * LN_CHUNK, LN_CHUNK), LN_CHUNK)
        y = y_ref[sl, :]
        mu = jnp.mean(y, axis=-1, keepdims=True)
        yc = y - mu
        var = jnp.mean(yc * yc, axis=-1, keepdims=True)
        out = yc * lax.rsqrt(var + LN_EPS) * g + b
        of_ref[sl, :] = out
        ob_ref[sl, :] = out.astype(BF16)
        return carry

    lax.fori_loop(0, y_ref.shape[0] // LN_CHUNK, body, 0)


def layer_norm(y, g, b, *, bm, name):
    m, dm = y.shape
    assert m % bm == 0 and bm % LN_CHUNK == 0
    row = pl.BlockSpec((bm, dm), lambda i: (i, 0))
    vec = pl.BlockSpec((1, dm), lambda i: (0, 0))
    return pl.pallas_call(
        _ln_kernel,
        grid=(m // bm,),
        in_specs=[row, vec, vec],
        out_specs=[row, row],
        out_shape=[jax.ShapeDtypeStruct((m, dm), F32), jax.ShapeDtypeStruct((m, dm), BF16)],
        compiler_params=_cparams(("arbitrary",)),
        name=name,
    )(y, g.reshape(1, dm), b.reshape(1, dm))


A_BATCH = 4
A_BLOCKS = 3


def _attn_a_prompt_kernel(seq, q0, q1, q2, k0, k1, k2, v0, v1, v2, o_ref, os0, os1, os2, ls0, ls1, ls2):
    qs, ks, vs = (q0, q1, q2), (k0, k1, k2), (v0, v1, v2)
    o_scr, l_scr = (os0, os1, os2), (ls0, ls1, ls2)
    qq = lax.broadcasted_iota(jnp.int32, (QB, QB), 0)
    kk = lax.broadcasted_iota(jnp.int32, (QB, QB), 1)
    causal = kk <= qq
    qq2 = lax.broadcasted_iota(jnp.int32, (QB, 2 * QB), 0)
    kk2 = lax.broadcasted_iota(jnp.int32, (QB, 2 * QB), 1)
    band = (kk2 >= qq2) & (kk2 <= qq2 + QB)

    def rows(start, n, d):
        return pl.ds(start, n) if d == 1 else pl.ds(start, n, stride=d)

    def emit(g, blocks, mask):
        s = [_dot_nt(qs[g][qr, :].astype(BF16), ks[g][kr, :].astype(BF16)) for qr, kr in blocks]
        p, m, l = [], [], []
        for x in s:
            x = jnp.where(mask, x * SCALE, -jnp.inf)
            mx = jnp.max(x, axis=-1, keepdims=True)
            e = jnp.exp(x - mx)
            m.append(mx)
            p.append(e.astype(BF16))
            l.append(jnp.sum(e, axis=-1, keepdims=True))
        o = [_dot(p[n], vs[g][kr, :].astype(BF16)) for n, (_, kr) in enumerate(blocks)]
        for n, (qr, _) in enumerate(blocks):
            o_scr[g][qr, :] = o[n] / l[n]
            l_scr[g][qr, :] = jnp.broadcast_to(m[n] + jnp.log(l[n]), (QB, LANES))

    for g, (window, d) in enumerate(DIL_GROUPS):
        assert window // d == QB
        sub_len = seq // d
        n_blk = sub_len // QB
        for r0 in range(0, d, A_BATCH):
            rs = range(r0, min(r0 + A_BATCH, d))
            emit(g, [(rows(r, QB, d), rows(r, QB, d)) for r in rs], causal)
        if n_blk == 1:
            continue
        if d >= A_BATCH:
            for r0 in range(0, d, A_BATCH):
                def body(c, carry, g=g, r0=r0, d=d):
                    start = pl.multiple_of(c * (QB * d), QB * d)
                    emit(g, [(rows(start + r, QB, d), rows(start - QB * d + r, 2 * QB, d))
                             for r in range(r0, r0 + A_BATCH)], band)
                    return carry
                lax.fori_loop(1, n_blk, body, 0)
        else:
            assert d == 1 and (n_blk - 1) % A_BLOCKS == 0
            def body(t, carry, g=g):
                start = pl.multiple_of((1 + t * A_BLOCKS) * QB, QB)
                emit(g, [(rows(start + n * QB, QB, 1), rows(start + (n - 1) * QB, 2 * QB, 1))
                         for n in range(A_BLOCKS)], band)
                return carry
            lax.fori_loop(0, (n_blk - 1) // A_BLOCKS, body, 0)

    def merge(c, carry):
        sl = pl.ds(pl.multiple_of(c * QB, QB), QB)
        l0, l1, l2 = ls0[sl, :], ls1[sl, :], ls2[sl, :]
        mx = jnp.maximum(jnp.maximum(l0, l1), l2)
        w0, w1, w2 = jnp.exp(l0 - mx), jnp.exp(l1 - mx), jnp.exp(l2 - mx)
        den = w0 + w1 + w2
        out = (w0 / den) * os0[sl, :] + (w1 / den) * os1[sl, :] + (w2 / den) * os2[sl, :]
        o_ref[sl, :] = out.astype(o_ref.dtype)
        return carry

    lax.fori_loop(0, seq // QB, merge, 0)


def attn_a_prompt(qkv, batch, seq):
    def spec(sec, g):
        base = (sec * N_GROUPS + g) * H_A
        return pl.BlockSpec((seq, HEAD_DIM), lambda b, h: (b, base + h))
    in_specs = [spec(sec, g) for sec in range(3) for g in range(N_GROUPS)]
    return pl.pallas_call(
        functools.partial(_attn_a_prompt_kernel, seq),
        grid=(batch, H_A),
        in_specs=in_specs,
        out_specs=pl.BlockSpec((seq, HEAD_DIM), lambda b, h: (b, h)),
        out_shape=jax.ShapeDtypeStruct((batch * seq, H_A * HEAD_DIM), BF16),
        scratch_shapes=[pltpu.VMEM((seq, HEAD_DIM), F32)] * (2 * N_GROUPS),
        compiler_params=_cparams(("arbitrary", "arbitrary")),
        name="attn_a_prompt",
    )(*([qkv] * 9))


def _attn_a_sample_kernel(qkv_ref, c0_ref, c1_ref, c2_ref, o_ref):
    b = pl.program_id(0)
    caches = (c0_ref, c1_ref, c2_ref)
    row = qkv_ref[pl.ds(b, 1), :]
    hw = H_A * HEAD_DIM

    def piece(sec, g, h):
        base = (sec * N_GROUPS + g) * hw + h * HEAD_DIM
        return row[:, base:base + HEAD_DIM]

    for h in range(H_A):
        outs, lses = [], []
        for g in range(N_GROUPS):
            q, kn, vn = piece(0, g, h), piece(1, g, h), piece(2, g, h)
            kc = caches[g][0, :, h * HEAD_DIM:(h + 1) * HEAD_DIM]
            vc = caches[g][0, :, hw + h * HEAD_DIM:hw + (h + 1) * HEAD_DIM]
            sc = jnp.sum(kc * q, axis=-1, keepdims=True) * SCALE
            sn = jnp.sum(kn * q, axis=-1, keepdims=True) * SCALE
            m = jnp.maximum(jnp.max(sc, axis=0, keepdims=True), sn)
            pc = jnp.exp(sc - m)
            pn = jnp.exp(sn - m)
            l = jnp.sum(pc, axis=0, keepdims=True) + pn
            o = (jnp.sum(pc * vc, axis=0, keepdims=True) + pn * vn) / l
            outs.append(o)
            lses.append(m + jnp.log(l))
        mx = jnp.maximum(jnp.maximum(lses[0], lses[1]), lses[2])
        ws = [jnp.exp(l - mx) for l in lses]
        den = ws[0] + ws[1] + ws[2]
        out = (ws[0] / den) * outs[0] + (ws[1] / den) * outs[1] + (ws[2] / den) * outs[2]
        o_ref[0, :, h * HEAD_DIM:(h + 1) * HEAD_DIM] = out


def attn_a_sample(qkv_s, caches):
    nb = caches[0].shape[0]
    hw2 = 2 * H_A * HEAD_DIM
    views, specs = [], []
    for g, (window, d) in enumerate(DIL_GROUPS):
        lbuf = caches[g].shape[1]
        assert lbuf == QB * d
        views.append(caches[g][:, ::d].reshape(nb, QB, hw2))
        specs.append(pl.BlockSpec((1, QB, hw2), lambda b: (b, 0, 0)))
    return pl.pallas_call(
        _attn_a_sample_kernel,
        grid=(nb,),
        in_specs=[pl.BlockSpec(qkv_s.shape, lambda b: (0, 0))] + specs,
        out_specs=pl.BlockSpec((1, 1, H_A * HEAD_DIM), lambda b: (b, 0, 0)),
        out_shape=jax.ShapeDtypeStruct((nb, 1, H_A * HEAD_DIM), F32),
        compiler_params=_cparams(("arbitrary",)),
        name="attn_a_sample",
    )(qkv_s, *views)


def _log_sigmoids(z):
    ls = jnp.minimum(z, 0.0) - jnp.log(1.0 + jnp.exp(-jnp.abs(z)))
    return ls, ls - z


def _split_bf16(x):
    hi = x.astype(BF16)
    lo = (x - hi.astype(F32)).astype(BF16)
    return hi, lo


def _later_keys_matrix(n):
    ji = lax.broadcasted_iota(jnp.int32, (2 * n, n), 0)
    si = lax.broadcasted_iota(jnp.int32, (2 * n, n), 1)
    return jnp.where((ji % n) > si, 1.0, 0.0).astype(BF16)


SB_T = 256
SB_HEADS = 4


def _attn_b_prompt_kernel(seq, bias_ref, q_ref, k_ref, v_ref, o_ref):
    g = pl.program_id(1)
    later = _later_keys_matrix(SB_T)
    qq = lax.broadcasted_iota(jnp.int32, (SB_T, SB_T), 0)
    kk = lax.broadcasted_iota(jnp.int32, (SB_T, SB_T), 1)
    before = kk < qq

    heads = range(SB_HEADS)
    cols = [slice(h * HEAD_DIM, (h + 1) * HEAD_DIM) for h in heads]

    def tiles(q0, k0, masked, state):
        z = [_dot_nt(q_ref[pl.ds(q0, SB_T), cols[h]], k_ref[pl.ds(k0, SB_T), cols[h]]) for h in heads]
        ls, lk, hilo = [], [], []
        for h in heads:
            s, k = _log_sigmoids(z[h] * SCALE + bias_ref[g * SB_HEADS + h])
            if masked:
                k = jnp.where(before, k, 0.0)
            ls.append(s)
            lk.append(k)
            hilo.append(jnp.concatenate(_split_bf16(k), axis=1))
        after = [_dot(hilo[h], later) for h in heads]
        a = []
        for h in heads:
            w = jnp.exp(ls[h] + after[h] + state[2 * h])
            if masked:
                w = jnp.where(before, w, 0.0)
            a.append(w.astype(BF16))
        out = []
        for h in heads:
            out.append(state[2 * h] + after[h][:, 0:1] + lk[h][:, 0:1])
            out.append(state[2 * h + 1] + _dot(a[h], v_ref[pl.ds(k0, SB_T), cols[h]]))
        return tuple(out)

    def q_body(qi, _):
        q0 = pl.multiple_of(qi * SB_T, SB_T)
        init = (jnp.zeros((SB_T, 1), F32), jnp.zeros((SB_T, HEAD_DIM), F32)) * SB_HEADS
        state = tiles(q0, q0, True, init)

        def k_body(t, st):
            return tiles(q0, pl.multiple_of((qi - 1 - t) * SB_T, SB_T), False, st)

        state = lax.fori_loop(0, qi, k_body, state)
        for h in heads:
            o_ref[pl.ds(q0, SB_T), cols[h]] = state[2 * h + 1].astype(o_ref.dtype)
        return 0

    lax.fori_loop(0, seq // SB_T, q_body, 0)


def attn_b_prompt(q, k, v, bias, batch, seq):
    n_heads = q.shape[1] // HEAD_DIM
    assert n_heads % SB_HEADS == 0 and seq % SB_T == 0
    spec = pl.BlockSpec((seq, SB_HEADS * HEAD_DIM), lambda b, g: (b, g))
    return pl.pallas_call(
        functools.partial(_attn_b_prompt_kernel, seq),
        grid=(batch, n_heads // SB_HEADS),
        in_specs=[pl.BlockSpec(memory_space=pltpu.SMEM), spec, spec, spec],
        out_specs=spec,
        out_shape=jax.ShapeDtypeStruct((batch * seq, n_heads * HEAD_DIM), BF16),
        compiler_params=_cparams(("arbitrary", "arbitrary")),
        name="attn_b_prompt",
    )(bias, q, k, v)


def _attn_b_sample_kernel(n_pages, n_heads, pt_ref, q_ref, bias_ref, k_ref, v_ref, o_ref,
                          qt_scr, acc_scr, carry_scr):
    b = pl.program_id(0)
    p = pl.program_id(1)
    page = PAGE_SIZE

    @pl.when(p == 0)
    def _():
        qrow = q_ref[pl.ds(b, 1), :]
        for h in range(n_heads):
            qt_scr[h:h + 1, :] = qrow[:, h * HEAD_DIM:(h + 1) * HEAD_DIM]
        acc_scr[...] = jnp.zeros_like(acc_scr)
        carry_scr[...] = jnp.zeros_like(carry_scr)

    shape3 = (page, n_heads, LANES)
    own = lax.broadcasted_iota(jnp.int32, shape3, 0) == lax.broadcasted_iota(jnp.int32, shape3, 2)
    s3 = jnp.sum(k_ref[...] * qt_scr[...][None], axis=-1, keepdims=True)
    z = jnp.sum(jnp.where(own, s3, 0.0), axis=0)
    z = z * SCALE + bias_ref[...]
    ls, lk = _log_sigmoids(z)
    hi, lo = _split_bf16(lk)
    after = _dot(jnp.concatenate([hi, lo], axis=1), _later_keys_matrix(page))
    a = jnp.exp(ls + after + carry_scr[...])
    carry_scr[...] += after[:, 0:1] + lk[:, 0:1]
    a3 = jnp.sum(jnp.where(own, a[None], 0.0), axis=-1, keepdims=True)
    acc_scr[...] += jnp.sum(a3 * v_ref[...], axis=0)

    @pl.when(p == n_pages - 1)
    def _():
        for h in range(n_heads):
            o_ref[0, :, h * HEAD_DIM:(h + 1) * HEAD_DIM] = acc_scr[h:h + 1, :]


def attn_b_sample(q_s, bias, cache_k, cache_v, page_table, layer):
    nb, n_pages = page_table.shape
    n_layers, n_pool, page, n_heads, dh = cache_k.shape
    dm = n_heads * dh
    assert page == PAGE_SIZE and page == LANES and dh == HEAD_DIM and n_heads % SUBLANES == 0
    page_spec = pl.BlockSpec((None, None, page, n_heads, dh),
                             lambda b, p, pt: (layer, pt[b * n_pages + (n_pages - 1 - p)], 0, 0, 0))
    grid_spec = pltpu.PrefetchScalarGridSpec(
        num_scalar_prefetch=1,
        grid=(nb, n_pages),
        in_specs=[pl.BlockSpec(q_s.shape, lambda b, p, pt: (0, 0)),
                  pl.BlockSpec((n_heads, 1), lambda b, p, pt: (0, 0)),
                  page_spec, page_spec],
        out_specs=pl.BlockSpec((1, 1, dm), lambda b, p, pt: (b, 0, 0)),
        scratch_shapes=[pltpu.VMEM((n_heads, dh), F32), pltpu.VMEM((n_heads, dh), F32),
                        pltpu.VMEM((n_heads, 1), F32)],
    )
    return pl.pallas_call(
        functools.partial(_attn_b_sample_kernel, n_pages, n_heads),
        grid_spec=grid_spec,
        out_shape=jax.ShapeDtypeStruct((nb, 1, dm), F32),
        compiler_params=_cparams(("arbitrary", "arbitrary")),
        name="attn_b_sample",
    )(page_table.reshape(-1), q_s, bias.reshape(n_heads, 1), cache_k, cache_v)


def _rope_tables(pos):
    half = ROT_DIM // 2
    inv_freq = ROPE_THETA ** (-jnp.arange(half, dtype=F32) / half)
    ang = pos.astype(F32)[:, None] * inv_freq[None, :]
    cos, sin = jnp.cos(ang), jnp.sin(ang)
    n = pos.shape[0]
    rest = HEAD_DIM - ROT_DIM
    c = jnp.concatenate([cos, cos, jnp.ones((n, rest), F32)], axis=1)
    s1 = jnp.concatenate([jnp.zeros((n, half), F32), sin, jnp.zeros((n, rest), F32)], axis=1)
    s2 = jnp.concatenate([-sin, jnp.zeros((n, half + rest), F32)], axis=1)
    return c, s1, s2


def _pad_rows(x, rows):
    return jnp.pad(x, ((0, rows - x.shape[0]), (0, 0)))


def kernel(x_prompt, x_sample, cache_a_g0, cache_a_g1, cache_a_g2, cache_b_k, cache_b_v, page_table, w_qkv_a, w_o_a, w_qkv_b, w_o_b, sb_bias, w_ffn_in, w_ffn_out, ln_mix_g, ln_mix_b, ln_ffn_g, ln_ffn_b):
    batch, seq, dm = x_prompt.shape
    nb, dec_seq, _ = x_sample.shape
    assert dec_seq == 1 and nb <= SAMPLE_ROWS
    mp = batch * seq
    caches_a = (cache_a_g0, cache_a_g1, cache_a_g2)
    hw = H_A * HEAD_DIM

    wqa = w_qkv_a.astype(BF16)
    woa = w_o_a.astype(BF16)
    wqb = w_qkv_b.astype(BF16)
    wob = w_o_b.astype(BF16)
    wout = w_ffn_out.astype(BF16)

    xp = x_prompt.reshape(mp, dm)
    xs = _pad_rows(x_sample.reshape(nb, dm), SAMPLE_ROWS)
    xp_b, xs_b = xp.astype(BF16), xs.astype(BF16)

    tabs_p = _rope_tables(jnp.tile(jnp.arange(seq), batch))
    tabs_s = _rope_tables(jnp.full((SAMPLE_ROWS,), PAST_LEN, jnp.int32))

    def residual_ln(a_p, a_s, w, xp, xs, g, b, *, bm, bn, name):
        (y_p,), (y_s,) = proj(a_p, a_s, w, bm=bm, bn=bn, out_dtypes=(F32,), resid=(xp, xs), name=name)
        xp, xp_b = layer_norm(y_p, g, b, bm=256, name="ln_prompt")
        xs, xs_b = layer_norm(y_s, g, b, bm=SAMPLE_ROWS, name="ln_sample")
        return xp, xp_b, xs, xs_b

    a_prompt = [[] for _ in range(N_GROUPS)]
    a_sample = [[] for _ in range(N_GROUPS)]
    bk_p, bv_p, bk_s, bv_s = [], [], [], []

    for i in range(DEPTH):
        j = i // 2
        if i % 2 == 0:
            (qkv_p,), (qkv_s,) = proj(xp_b, xs_b, wqa[j], bm=1024, bn=768, out_dtypes=(F32,),
                                      rope=(tabs_p, tabs_s, 2 * N_GROUPS * hw), name="qkv_a")
            att_p = attn_a_prompt(qkv_p, batch, seq)
            att_s = attn_a_sample(qkv_s, [c[j] for c in caches_a])
            att_s = _pad_rows(att_s.reshape(nb, hw), SAMPLE_ROWS).astype(BF16)
            for g, (window, _) in enumerate(DIL_GROUPS):
                keep = min(window, seq)
                kcol, vcol = (N_GROUPS + g) * hw, (2 * N_GROUPS + g) * hw
                q3 = qkv_p.reshape(batch, seq, -1)
                kp = q3[:, seq - keep:, kcol:kcol + hw].reshape(batch, keep, H_A, HEAD_DIM)
                vp = q3[:, seq - keep:, vcol:vcol + hw].reshape(batch, keep, H_A, HEAD_DIM)
                a_prompt[g].append(jnp.stack([kp, vp], axis=2))
                kn = qkv_s[:nb, kcol:kcol + hw].reshape(nb, 1, 1, H_A, HEAD_DIM)
                vn = qkv_s[:nb, vcol:vcol + hw].reshape(nb, 1, 1, H_A, HEAD_DIM)
                new = jnp.concatenate([kn, vn], axis=2)
                a_sample[g].append(jnp.concatenate([caches_a[g][j][:, 1:], new], axis=1))
            xp, xp_b, xs, xs_b = residual_ln(att_p, att_s, woa[j], xp, xs, ln_mix_g[i], ln_mix_b[i],
                                             bm=1024, bn=1024, name="o_a")
        else:
            secs = [proj(xp_b, xs_b, wqb[j], col_off=sec * dm, n_cols=dm, bm=1024, bn=1024,
                         out_dtypes=(F32, BF16), name="qkv_b") for sec in range(3)]
            ((_, q_pb), (q_s, _)), ((k_p, k_pb), (k_s, _)), ((v_p, v_pb), (v_s, _)) = secs
            att_p = attn_b_prompt(q_pb, k_pb, v_pb, sb_bias[j], batch, seq)
            att_s = attn_b_sample(q_s, sb_bias[j], cache_b_k, cache_b_v, page_table, j)
            att_s = _pad_rows(att_s.reshape(nb, dm), SAMPLE_ROWS).astype(BF16)
            bk_p.append(k_p.reshape(batch, seq, H_B, HEAD_DIM))
            bv_p.append(v_p.reshape(batch, seq, H_B, HEAD_DIM))
            bk_s.append(k_s[:nb].reshape(nb, 1, H_B, HEAD_DIM))
            bv_s.append(v_s[:nb].reshape(nb, 1, H_B, HEAD_DIM))
            xp, xp_b, xs, xs_b = residual_ln(att_p, att_s, wob[j], xp, xs, ln_mix_g[i], ln_mix_b[i],
                                             bm=1024, bn=512, name="o_b")
        act_p, act_s = swiglu_in(xp_b, xs_b, w_ffn_in, i, bm=1024, bn=256, name="ffn_in")
        xp, xp_b, xs, xs_b = residual_ln(act_p, act_s, wout[i], xp, xs, ln_ffn_g[i], ln_ffn_b[i],
                                         bm=512, bn=512, name="ffn_out")

    y_prompt = xp.reshape(batch, seq, dm)
    y_sample = xs[:nb].reshape(nb, 1, dm)
    st = lambda parts: jnp.stack(parts, axis=0)
    return (y_prompt, y_sample,
            st(a_prompt[0]), st(a_sample[0]), st(a_prompt[1]), st(a_sample[1]), st(a_prompt[2]), st(a_sample[2]),
            st(bk_p), st(bv_p), st(bk_s), st(bv_s))
```

```python
import functools
import math

import jax
import jax.numpy as jnp
from jax import lax
from jax.experimental import pallas as pl
from jax.experimental.pallas import tpu as pltpu

F32 = jnp.float32
BF16 = jnp.bfloat16

D_MODEL = 4096
DEPTH = 4
PAST_LEN = 8192
PAGE_SIZE = 128
HEAD_DIM = 128
DIL_GROUPS = ((128, 1), (512, 4), (2048, 16))
N_GROUPS = len(DIL_GROUPS)
H_A = 10
H_B = D_MODEL // HEAD_DIM
ROT_DIM = HEAD_DIM // 4
ROPE_THETA = 500000.0
D_FF = -(-8 * D_MODEL // (3 * 256)) * 256
QB = 128
ALPHA = (2.0 * DEPTH) ** 0.25
LN_EPS = 1e-5
SCALE = 1.0 / math.sqrt(HEAD_DIM)

LANES = 128
SUBLANES = 8
SAMPLE_ROWS = 16
VMEM_LIMIT = 56 * 1024 * 1024


def _cparams(sem):
    return pltpu.CompilerParams(dimension_semantics=sem, vmem_limit_bytes=VMEM_LIMIT)


def _dot(a, b):
    return jnp.dot(a, b, preferred_element_type=F32)


def _dot_nt(a, b):
    return lax.dot_general(a, b, (((1,), (1,)), ((), ())), preferred_element_type=F32)


def _rope(x, c, s1, s2):
    return x * c + pltpu.roll(x, ROT_DIM // 2, 1) * s1 + pltpu.roll(x, LANES - ROT_DIM // 2, 1) * s2


def _proj_kernel(n_row_tiles, n_rope_tiles, has_resid, n_out, *refs):
    xp_ref, xs_ref, w_ref = refs[:3]
    pos = 3
    if has_resid:
        rp_ref, rs_ref = refs[pos:pos + 2]
        pos += 2
    if n_rope_tiles:
        tabs_p, tabs_s = refs[pos:pos + 3], refs[pos + 3:pos + 6]
        pos += 6
    outs_p, outs_s = refs[pos:pos + n_out], refs[pos + n_out:pos + 2 * n_out]
    j = pl.program_id(0)
    i = pl.program_id(1)

    def emit(x_ref, r_ref, tabs, outs):
        y = _dot(x_ref[...], w_ref[...])
        if has_resid:
            y = ALPHA * r_ref[...] + y

        def plain():
            for o in outs:
                o[...] = y.astype(o.dtype)

        if not n_rope_tiles:
            plain()
            return

        @pl.when(j < n_rope_tiles)
        def _():
            c, s1, s2 = tabs[0][...], tabs[1][...], tabs[2][...]
            for t in range(y.shape[1] // LANES):
                r = _rope(y[:, t * LANES:(t + 1) * LANES], c, s1, s2)
                for o in outs:
                    o[:, t * LANES:(t + 1) * LANES] = r.astype(o.dtype)

        pl.when(j >= n_rope_tiles)(plain)

    @pl.when(i < n_row_tiles)
    def _():
        emit(xp_ref, rp_ref if has_resid else None, tabs_p if n_rope_tiles else None, outs_p)

    @pl.when(i == n_row_tiles)
    def _():
        emit(xs_ref, rs_ref if has_resid else None, tabs_s if n_rope_tiles else None, outs_s)


def proj(xp, xs, w, *, bm, bn, out_dtypes, name, col_off=0, n_cols=None, resid=None, rope=None):
    mp, kd = xp.shape
    ms = xs.shape[0]
    n_cols = w.shape[1] if n_cols is None else n_cols
    nI, nJ = mp // bm, n_cols // bn
    assert nI * bm == mp and nJ * bn == n_cols and col_off % bn == 0 and bn % LANES == 0
    joff = col_off // bn
    row_p = lambda j, i: (jnp.minimum(i, nI - 1), 0)
    tile_p = lambda j, i: (jnp.minimum(i, nI - 1), j)
    fixed = lambda j, i: (0, 0)
    tile_s = lambda j, i: (0, j)
    args = [xp, xs, w]
    in_specs = [pl.BlockSpec((bm, kd), row_p), pl.BlockSpec((ms, kd), fixed),
                pl.BlockSpec((kd, bn), lambda j, i: (0, j + joff))]
    if resid is not None:
        args += list(resid)
        in_specs += [pl.BlockSpec((bm, bn), tile_p), pl.BlockSpec((ms, bn), tile_s)]
    n_rope_tiles = 0
    if rope is not None:
        tabs_p, tabs_s, n_rope_cols = rope
        assert n_rope_cols % bn == 0
        n_rope_tiles = n_rope_cols // bn
        args += list(tabs_p) + list(tabs_s)
        in_specs += [pl.BlockSpec((bm, LANES), row_p)] * 3 + [pl.BlockSpec((ms, LANES), fixed)] * 3
    n_out = len(out_dtypes)
    outs = pl.pallas_call(
        functools.partial(_proj_kernel, nI, n_rope_tiles, resid is not None, n_out),
        grid=(nJ, nI + 1),
        in_specs=in_specs,
        out_specs=[pl.BlockSpec((bm, bn), tile_p)] * n_out + [pl.BlockSpec((ms, bn), tile_s)] * n_out,
        out_shape=[jax.ShapeDtypeStruct((mp, n_cols), dt) for dt in out_dtypes]
                  + [jax.ShapeDtypeStruct((ms, n_cols), dt) for dt in out_dtypes],
        compiler_params=_cparams(("arbitrary", "arbitrary")),
        name=name,
    )(*args)
    return outs[:n_out], outs[n_out:]


W_CAST_ROWS = 512


def _swiglu_kernel(n_row_tiles, xp_ref, xs_ref, wg_ref, wu_ref, yp_ref, ys_ref, wgb_scr, wub_scr):
    i = pl.program_id(1)

    @pl.when(i == 0)
    def _():
        for r in range(0, wg_ref.shape[0], W_CAST_ROWS):
            wgb_scr[r:r + W_CAST_ROWS, :] = wg_ref[r:r + W_CAST_ROWS, :].astype(BF16)
            wub_scr[r:r + W_CAST_ROWS, :] = wu_ref[r:r + W_CAST_ROWS, :].astype(BF16)

    def act(x):
        g = _dot(x, wgb_scr[...])
        u = _dot(x, wub_scr[...])
        return ((g * (1.0 / (1.0 + jnp.exp(-g)))) * u).astype(yp_ref.dtype)

    @pl.when(i < n_row_tiles)
    def _():
        yp_ref[...] = act(xp_ref[...])

    @pl.when(i == n_row_tiles)
    def _():
        ys_ref[...] = act(xs_ref[...])


def swiglu_in(xp, xs, w_in, layer, *, bm, bn, name):
    mp, kd = xp.shape
    ms = xs.shape[0]
    ff = w_in.shape[2] // 2
    nI, nJ = mp // bm, ff // bn
    assert nI * bm == mp and nJ * bn == ff and kd % W_CAST_ROWS == 0
    return pl.pallas_call(
        functools.partial(_swiglu_kernel, nI),
        grid=(nJ, nI + 1),
        in_specs=[
            pl.BlockSpec((bm, kd), lambda j, i: (jnp.minimum(i, nI - 1), 0)),
            pl.BlockSpec((ms, kd), lambda j, i: (0, 0)),
            pl.BlockSpec((None, kd, bn), lambda j, i: (layer, 0, j)),
            pl.BlockSpec((None, kd, bn), lambda j, i: (layer, 0, j + nJ)),
        ],
        out_specs=[
            pl.BlockSpec((bm, bn), lambda j, i: (jnp.minimum(i, nI - 1), j)),
            pl.BlockSpec((ms, bn), lambda j, i: (0, j)),
        ],
        out_shape=[jax.ShapeDtypeStruct((mp, ff), BF16), jax.ShapeDtypeStruct((ms, ff), BF16)],
        scratch_shapes=[pltpu.VMEM((kd, bn), BF16), pltpu.VMEM((kd, bn), BF16)],
        compiler_params=_cparams(("arbitrary", "arbitrary")),
        name=name,
    )(xp, xs, w_in, w_in)


LN_CHUNK = 16


def _ln_kernel(y_ref, g_ref, b_ref, of_ref, ob_ref):
    g = g_ref[...]
    b = b_ref[...]

    def body(c, carry):
        sl = pl.ds(pl.multiple_of(c * LN_CHUNK, LN_CHUNK), LN_CHUNK)
        y = y_ref[sl, :]
        mu = jnp.mean(y, axis=-1, keepdims=True)
        yc = y - mu
        var = jnp.mean(yc * yc, axis=-1, keepdims=True)
        out = yc * lax.rsqrt(var + LN_EPS) * g + b
        of_ref[sl, :] = out
        ob_ref[sl, :] = out.astype(BF16)
        return carry

    lax.fori_loop(0, y_ref.shape[0] // LN_CHUNK, body, 0)


def layer_norm(y, g, b, *, bm, name):
    m, dm = y.shape
    assert m % bm == 0 and bm % LN_CHUNK == 0
    row = pl.BlockSpec((bm, dm), lambda i: (i, 0))
    vec = pl.BlockSpec((1, dm), lambda i: (0, 0))
    return pl.pallas_call(
        _ln_kernel,
        grid=(m // bm,),
        in_specs=[row, vec, vec],
        out_specs=[row, row],
        out_shape=[jax.ShapeDtypeStruct((m, dm), F32), jax.ShapeDtypeStruct((m, dm), BF16)],
        compiler_params=_cparams(("arbitrary",)),
        name=name,
    )(y, g.reshape(1, dm), b.reshape(1, dm))


A_BATCH = 4
A_BLOCKS = 3


def _attn_a_prompt_kernel(seq, q0, q1, q2, k0, k1, k2, v0, v1, v2, o_ref, os0, os1, os2, ls0, ls1, ls2):
    qs, ks, vs = (q0, q1, q2), (k0, k1, k2), (v0, v1, v2)
    o_scr, l_scr = (os0, os1, os2), (ls0, ls1, ls2)
    qq = lax.broadcasted_iota(jnp.int32, (QB, QB), 0)
    kk = lax.broadcasted_iota(jnp.int32, (QB, QB), 1)
    causal = kk <= qq
    qq2 = lax.broadcasted_iota(jnp.int32, (QB, 2 * QB), 0)
    kk2 = lax.broadcasted_iota(jnp.int32, (QB, 2 * QB), 1)
    band = (kk2 >= qq2) & (kk2 <= qq2 + QB)

    def rows(start, n, d):
        return pl.ds(start, n) if d == 1 else pl.ds(start, n, stride=d)

    def emit(g, blocks, mask):
        s = [_dot_nt(qs[g][qr, :].astype(BF16), ks[g][kr, :].astype(BF16)) for qr, kr in blocks]
        p, m, l = [], [], []
        for x in s:
            x = jnp.where(mask, x * SCALE, -jnp.inf)
            mx = jnp.max(x, axis=-1, keepdims=True)
            e = jnp.exp(x - mx)
            m.append(mx)
            p.append(e.astype(BF16))
            l.append(jnp.sum(e, axis=-1, keepdims=True))
        o = [_dot(p[n], vs[g][kr, :].astype(BF16)) for n, (_, kr) in enumerate(blocks)]
        for n, (qr, _) in enumerate(blocks):
            o_scr[g][qr, :] = o[n] / l[n]
            l_scr[g][qr, :] = jnp.broadcast_to(m[n] + jnp.log(l[n]), (QB, LANES))

    for g, (window, d) in enumerate(DIL_GROUPS):
        assert window // d == QB
        sub_len = seq // d
        n_blk = sub_len // QB
        for r0 in range(0, d, A_BATCH):
            rs = range(r0, min(r0 + A_BATCH, d))
            emit(g, [(rows(r, QB, d), rows(r, QB, d)) for r in rs], causal)
        if n_blk == 1:
            continue
        if d >= A_BATCH:
            for r0 in range(0, d, A_BATCH):
                def body(c, carry, g=g, r0=r0, d=d):
                    start = pl.multiple_of(c * (QB * d), QB * d)
                    emit(g, [(rows(start + r, QB, d), rows(start - QB * d + r, 2 * QB, d))
                             for r in range(r0, r0 + A_BATCH)], band)
                    return carry
                lax.fori_loop(1, n_blk, body, 0)
        else:
            assert d == 1 and (n_blk - 1) % A_BLOCKS == 0
            def body(t, carry, g=g):
                start = pl.multiple_of((1 + t * A_BLOCKS) * QB, QB)
                emit(g, [(rows(start + n * QB, QB, 1), rows(start + (n - 1) * QB, 2 * QB, 1))
                         for n in range(A_BLOCKS)], band)
                return carry
            lax.fori_loop(0, (n_blk - 1) // A_BLOCKS, body, 0)

    def merge(c, carry):
        sl = pl.ds(pl.multiple_of(c * QB, QB), QB)
        l0, l1, l2 = ls0[sl, :], ls1[sl, :], ls2[sl, :]
        mx = jnp.maximum(jnp.maximum(l0, l1), l2)
        w0, w1, w2 = jnp.exp(l0 - mx), jnp.exp(l1 - mx), jnp.exp(l2 - mx)
        den = w0 + w1 + w2
        out = (w0 / den) * os0[sl, :] + (w1 / den) * os1[sl, :] + (w2 / den) * os2[sl, :]
        o_ref[sl, :] = out.astype(o_ref.dtype)
        return carry

    lax.fori_loop(0, seq // QB, merge, 0)


def attn_a_prompt(qkv, batch, seq):
    def spec(sec, g):
        base = (sec * N_GROUPS + g) * H_A
        return pl.BlockSpec((seq, HEAD_DIM), lambda b, h: (b, base + h))
    in_specs = [spec(sec, g) for sec in range(3) for g in range(N_GROUPS)]
    return pl.pallas_call(
        functools.partial(_attn_a_prompt_kernel, seq),
        grid=(batch, H_A),
        in_specs=in_specs,
        out_specs=pl.BlockSpec((seq, HEAD_DIM), lambda b, h: (b, h)),
        out_shape=jax.ShapeDtypeStruct((batch * seq, H_A * HEAD_DIM), BF16),
        scratch_shapes=[pltpu.VMEM((seq, HEAD_DIM), F32)] * (2 * N_GROUPS),
        compiler_params=_cparams(("arbitrary", "arbitrary")),
        name="attn_a_prompt",
    )(*([qkv] * 9))


KV_ROWS = 2 * H_A
KV_SLOT = -(-KV_ROWS // SUBLANES) * SUBLANES


def _attn_a_sample_kernel(layer, n_seq, lens, qkv_ref, c0_ref, c1_ref, c2_ref, o_ref, b0, b1, b2, sems):
    b = pl.program_id(0)
    caches, bufs = (c0_ref, c1_ref, c2_ref), (b0, b1, b2)
    copies = []
    for g, (_, d) in enumerate(DIL_GROUPS):
        rows = lens[g] * KV_ROWS
        seq_rows = caches[g].at[pl.ds(pl.multiple_of((layer * n_seq + b) * rows, SUBLANES), rows), :]
        for m in range(QB):
            copies.append(pltpu.make_async_copy(seq_rows.at[pl.ds(m * d * KV_ROWS, KV_ROWS), :],
                                                bufs[g].at[pl.ds(m * KV_SLOT, KV_ROWS), :], sems.at[g]))
    for c in copies:
        c.start()
    for c in copies:
        c.wait()

    row = qkv_ref[pl.ds(b, 1), :]
    hw = H_A * HEAD_DIM

    def piece(sec, g, h):
        base = (sec * N_GROUPS + g) * hw + h * HEAD_DIM
        return row[:, base:base + HEAD_DIM]

    for h in range(H_A):
        outs, lses = [], []
        for g in range(N_GROUPS):
            q, kn, vn = piece(0, g, h), piece(1, g, h), piece(2, g, h)
            kc = bufs[g][pl.ds(2 * h, QB, stride=KV_SLOT), :]
            vc = bufs[g][pl.ds(2 * h + 1, QB, stride=KV_SLOT), :]
            sc = jnp.sum(kc * q, axis=-1, keepdims=True) * SCALE
            sn = jnp.sum(kn * q, axis=-1, keepdims=True) * SCALE
            m = jnp.maximum(jnp.max(sc, axis=0, keepdims=True), sn)
            pc = jnp.exp(sc - m)
            pn = jnp.exp(sn - m)
            l = jnp.sum(pc, axis=0, keepdims=True) + pn
            o = (jnp.sum(pc * vc, axis=0, keepdims=True) + pn * vn) / l
            outs.append(o)
            lses.append(m + jnp.log(l))
        mx = jnp.maximum(jnp.maximum(lses[0], lses[1]), lses[2])
        ws = [jnp.exp(l - mx) for l in lses]
        den = ws[0] + ws[1] + ws[2]
        out = (ws[0] / den) * outs[0] + (ws[1] / den) * outs[1] + (ws[2] / den) * outs[2]
        o_ref[0, :, h * HEAD_DIM:(h + 1) * HEAD_DIM] = out


def _storage_rows(cache):
    lead = cache.shape[:-3]
    perm = tuple(range(len(lead))) + (len(lead) + 1, len(lead), len(lead) + 2)
    return cache.transpose(perm).reshape(-1, HEAD_DIM)


def attn_a_sample(qkv_s, caches, layer):
    nb = caches[0].shape[1]
    lens = tuple(c.shape[2] for c in caches)
    assert all(l == QB * d for l, (_, d) in zip(lens, DIL_GROUPS))
    return pl.pallas_call(
        functools.partial(_attn_a_sample_kernel, layer, nb, lens),
        grid=(nb,),
        in_specs=[pl.BlockSpec(qkv_s.shape, lambda b: (0, 0))] + [pl.BlockSpec(memory_space=pl.ANY)] * N_GROUPS,
        out_specs=pl.BlockSpec((1, 1, H_A * HEAD_DIM), lambda b: (b, 0, 0)),
        out_shape=jax.ShapeDtypeStruct((nb, 1, H_A * HEAD_DIM), F32),
        scratch_shapes=[pltpu.VMEM((QB * KV_SLOT, HEAD_DIM), F32)] * N_GROUPS
                       + [pltpu.SemaphoreType.DMA((N_GROUPS,))],
        compiler_params=_cparams(("arbitrary",)),
        name="attn_a_sample",
    )(qkv_s, *[_storage_rows(c) for c in caches])


def _log_sigmoids(z):
    ls = jnp.minimum(z, 0.0) - jnp.log(1.0 + jnp.exp(-jnp.abs(z)))
    return ls, ls - z


def _split_bf16(x):
    hi = x.astype(BF16)
    lo = (x - hi.astype(F32)).astype(BF16)
    return hi, lo


def _later_keys_matrix(n):
    ji = lax.broadcasted_iota(jnp.int32, (2 * n, n), 0)
    si = lax.broadcasted_iota(jnp.int32, (2 * n, n), 1)
    return jnp.where((ji % n) > si, 1.0, 0.0).astype(BF16)


SB_T = 256
SB_HEADS = 4


def _attn_b_prompt_kernel(seq, bias_ref, q_ref, k_ref, v_ref, o_ref):
    g = pl.program_id(1)
    later = _later_keys_matrix(SB_T)
    qq = lax.broadcasted_iota(jnp.int32, (SB_T, SB_T), 0)
    kk = lax.broadcasted_iota(jnp.int32, (SB_T, SB_T), 1)
    before = kk < qq

    heads = range(SB_HEADS)
    cols = [slice(h * HEAD_DIM, (h + 1) * HEAD_DIM) for h in heads]

    def tiles(q0, k0, masked, state):
        z = [_dot_nt(q_ref[pl.ds(q0, SB_T), cols[h]], k_ref[pl.ds(k0, SB_T), cols[h]]) for h in heads]
        ls, lk, hilo = [], [], []
        for h in heads:
            s, k = _log_sigmoids(z[h] * SCALE + bias_ref[g * SB_HEADS + h])
            if masked:
                k = jnp.where(before, k, 0.0)
            ls.append(s)
            lk.append(k)
            hilo.append(jnp.concatenate(_split_bf16(k), axis=1))
        after = [_dot(hilo[h], later) for h in heads]
        a = []
        for h in heads:
            w = jnp.exp(ls[h] + after[h] + state[2 * h])
            if masked:
                w = jnp.where(before, w, 0.0)
            a.append(w.astype(BF16))
        out = []
        for h in heads:
            out.append(state[2 * h] + after[h][:, 0:1] + lk[h][:, 0:1])
            out.append(state[2 * h + 1] + _dot(a[h], v_ref[pl.ds(k0, SB_T), cols[h]]))
        return tuple(out)

    def q_body(qi, _):
        q0 = pl.multiple_of(qi * SB_T, SB_T)
        init = (jnp.zeros((SB_T, 1), F32), jnp.zeros((SB_T, HEAD_DIM), F32)) * SB_HEADS
        state = tiles(q0, q0, True, init)

        def k_body(t, st):
            return tiles(q0, pl.multiple_of((qi - 1 - t) * SB_T, SB_T), False, st)

        state = lax.fori_loop(0, qi, k_body, state)
        for h in heads:
            o_ref[pl.ds(q0, SB_T), cols[h]] = state[2 * h + 1].astype(o_ref.dtype)
        return 0

    lax.fori_loop(0, seq // SB_T, q_body, 0)


def attn_b_prompt(q, k, v, bias, batch, seq):
    n_heads = q.shape[1] // HEAD_DIM
    assert n_heads % SB_HEADS == 0 and seq % SB_T == 0
    spec = pl.BlockSpec((seq, SB_HEADS * HEAD_DIM), lambda b, g: (b, g))
    return pl.pallas_call(
        functools.partial(_attn_b_prompt_kernel, seq),
        grid=(batch, n_heads // SB_HEADS),
        in_specs=[pl.BlockSpec(memory_space=pltpu.SMEM), spec, spec, spec],
        out_specs=spec,
        out_shape=jax.ShapeDtypeStruct((batch * seq, n_heads * HEAD_DIM), BF16),
        compiler_params=_cparams(("arbitrary", "arbitrary")),
        name="attn_b_prompt",
    )(bias, q, k, v)


def _attn_b_sample_kernel(n_pages, n_heads, pt_ref, q_ref, bias_ref, k_ref, v_ref, o_ref,
                          qt_scr, acc_scr, carry_scr):
    b = pl.program_id(0)
    p = pl.program_id(1)
    page = PAGE_SIZE

    @pl.when(p == 0)
    def _():
        qrow = q_ref[pl.ds(b, 1), :]
        for h in range(n_heads):
            qt_scr[h:h + 1, :] = qrow[:, h * HEAD_DIM:(h + 1) * HEAD_DIM]
        acc_scr[...] = jnp.zeros_like(acc_scr)
        carry_scr[...] = jnp.zeros_like(carry_scr)

    shape3 = (page, n_heads, LANES)
    own = lax.broadcasted_iota(jnp.int32, shape3, 0) == lax.broadcasted_iota(jnp.int32, shape3, 2)
    s3 = jnp.sum(k_ref[...] * qt_scr[...][None], axis=-1, keepdims=True)
    z = jnp.sum(jnp.where(own, s3, 0.0), axis=0)
    z = z * SCALE + bias_ref[...]
    ls, lk = _log_sigmoids(z)
    hi, lo = _split_bf16(lk)
    after = _dot(jnp.concatenate([hi, lo], axis=1), _later_keys_matrix(page))
    a = jnp.exp(ls + after + carry_scr[...])
    carry_scr[...] += after[:, 0:1] + lk[:, 0:1]
    a3 = jnp.sum(jnp.where(own, a[None], 0.0), axis=-1, keepdims=True)
    acc_scr[...] += jnp.sum(a3 * v_ref[...], axis=0)

    @pl.when(p == n_pages - 1)
    def _():
        for h in range(n_heads):
            o_ref[0, :, h * HEAD_DIM:(h + 1) * HEAD_DIM] = acc_scr[h:h + 1, :]


def attn_b_sample(q_s, bias, cache_k, cache_v, page_table, layer):
    nb, n_pages = page_table.shape
    n_layers, n_pool, page, n_heads, dh = cache_k.shape
    dm = n_heads * dh
    assert page == PAGE_SIZE and page == LANES and dh == HEAD_DIM and n_heads % SUBLANES == 0
    page_spec = pl.BlockSpec((None, None, page, n_heads, dh),
                             lambda b, p, pt: (layer, pt[b * n_pages + (n_pages - 1 - p)], 0, 0, 0))
    grid_spec = pltpu.PrefetchScalarGridSpec(
        num_scalar_prefetch=1,
        grid=(nb, n_pages),
        in_specs=[pl.BlockSpec(q_s.shape, lambda b, p, pt: (0, 0)),
                  pl.BlockSpec((n_heads, 1), lambda b, p, pt: (0, 0)),
                  page_spec, page_spec],
        out_specs=pl.BlockSpec((1, 1, dm), lambda b, p, pt: (b, 0, 0)),
        scratch_shapes=[pltpu.VMEM((n_heads, dh), F32), pltpu.VMEM((n_heads, dh), F32),
                        pltpu.VMEM((n_heads, 1), F32)],
    )
    return pl.pallas_call(
        functools.partial(_attn_b_sample_kernel, n_pages, n_heads),
        grid_spec=grid_spec,
        out_shape=jax.ShapeDtypeStruct((nb, 1, dm), F32),
        compiler_params=_cparams(("arbitrary", "arbitrary")),
        name="attn_b_sample",
    )(page_table.reshape(-1), q_s, bias.reshape(n_heads, 1), cache_k, cache_v)


def _kv_interleave_kernel(n_layers, *refs):
    o_ref = refs[-1]
    layer = pl.program_id(0)
    for l in range(n_layers):
        k_ref, v_ref = refs[2 * l], refs[2 * l + 1]

        @pl.when(layer == l)
        def _():
            n = k_ref.shape[0]
            for h in range(H_A):
                cols = slice(h * HEAD_DIM, (h + 1) * HEAD_DIM)
                o_ref[pl.ds(2 * h, n, stride=KV_ROWS), :] = k_ref[:, cols]
                o_ref[pl.ds(2 * h + 1, n, stride=KV_ROWS), :] = v_ref[:, cols]


def kv_interleave(qkvs, g, n_seq, seq, keep, *, bt=128):
    assert keep % bt == 0 and seq % bt == 0
    hw = H_A * HEAD_DIM
    tiles = keep // bt
    first = (seq - keep) // bt
    n_layers = len(qkvs)

    def slab(sec):
        return pl.BlockSpec((bt, hw), lambda l, b, i: (b * (seq // bt) + first + i, sec * N_GROUPS + g))

    return pl.pallas_call(
        functools.partial(_kv_interleave_kernel, n_layers),
        grid=(n_layers, n_seq, tiles),
        in_specs=[slab(1), slab(2)] * n_layers,
        out_specs=pl.BlockSpec((bt * KV_ROWS, HEAD_DIM), lambda l, b, i: ((l * n_seq + b) * tiles + i, 0)),
        out_shape=jax.ShapeDtypeStruct((n_layers * n_seq * keep * KV_ROWS, HEAD_DIM), F32),
        compiler_params=_cparams(("arbitrary", "arbitrary", "arbitrary")),
        name="kv_interleave",
    )(*[q for q in qkvs for _ in range(2)])


def _from_storage_rows(rows, lead):
    n = len(lead)
    return rows.reshape(tuple(lead) + (H_A, 2, HEAD_DIM)).transpose(tuple(range(n)) + (n + 1, n, n + 2))


def _cache_shift_kernel(n_seq, lens, *refs):
    n = len(lens)
    olds, news, outs, sems = refs[:n], refs[n:2 * n], refs[2 * n:3 * n], refs[3 * n]
    for g in range(n):
        rows = lens[g] * KV_ROWS
        copies = []
        for s in range(n_seq):
            base = s * rows
            copies.append(pltpu.make_async_copy(
                olds[g].at[pl.ds(base + KV_ROWS, rows - KV_ROWS), :],
                outs[g].at[pl.ds(base, rows - KV_ROWS), :], sems.at[2 * s]))
            copies.append(pltpu.make_async_copy(
                news[g].at[pl.ds(s * KV_ROWS, KV_ROWS), :],
                outs[g].at[pl.ds(base + rows - KV_ROWS, KV_ROWS), :], sems.at[2 * s + 1]))
        for c in copies:
            c.start()
        for c in copies:
            c.wait()


def cache_shift(olds, news):
    n_seq = olds[0].shape[0]
    lens = tuple(o.shape[1] for o in olds)
    old2 = [_storage_rows(o) for o in olds]
    new2 = [_storage_rows(x) for x in news]
    any_spec = pl.BlockSpec(memory_space=pl.ANY)
    outs = pl.pallas_call(
        functools.partial(_cache_shift_kernel, n_seq, lens),
        in_specs=[any_spec] * (2 * len(olds)),
        out_specs=[any_spec] * len(olds),
        out_shape=[jax.ShapeDtypeStruct(o.shape, F32) for o in old2],
        scratch_shapes=[pltpu.SemaphoreType.DMA((2 * n_seq,))],
        name="cache_shift",
    )(*old2, *new2)
    return [_from_storage_rows(o, (n_seq, l)) for o, l in zip(outs, lens)]


def _rope_tables(pos):
    half = ROT_DIM // 2
    inv_freq = ROPE_THETA ** (-jnp.arange(half, dtype=F32) / half)
    ang = pos.astype(F32)[:, None] * inv_freq[None, :]
    cos, sin = jnp.cos(ang), jnp.sin(ang)
    n = pos.shape[0]
    rest = HEAD_DIM - ROT_DIM
    c = jnp.concatenate([cos, cos, jnp.ones((n, rest), F32)], axis=1)
    s1 = jnp.concatenate([jnp.zeros((n, half), F32), sin, jnp.zeros((n, rest), F32)], axis=1)
    s2 = jnp.concatenate([-sin, jnp.zeros((n, half + rest), F32)], axis=1)
    return c, s1, s2


def _pad_rows(x, rows):
    return jnp.pad(x, ((0, rows - x.shape[0]), (0, 0)))


def kernel(x_prompt, x_sample, cache_a_g0, cache_a_g1, cache_a_g2, cache_b_k, cache_b_v, page_table, w_qkv_a, w_o_a, w_qkv_b, w_o_b, sb_bias, w_ffn_in, w_ffn_out, ln_mix_g, ln_mix_b, ln_ffn_g, ln_ffn_b):
    batch, seq, dm = x_prompt.shape
    nb, dec_seq, _ = x_sample.shape
    assert dec_seq == 1 and nb <= SAMPLE_ROWS
    mp = batch * seq
    caches_a = (cache_a_g0, cache_a_g1, cache_a_g2)
    hw = H_A * HEAD_DIM

    wqa = w_qkv_a.astype(BF16)
    woa = w_o_a.astype(BF16)
    wqb = w_qkv_b.astype(BF16)
    wob = w_o_b.astype(BF16)
    wout = w_ffn_out.astype(BF16)

    xp = x_prompt.reshape(mp, dm)
    xs = _pad_rows(x_sample.reshape(nb, dm), SAMPLE_ROWS)
    xp_b, xs_b = xp.astype(BF16), xs.astype(BF16)

    tabs_p = _rope_tables(jnp.tile(jnp.arange(seq), batch))
    tabs_s = _rope_tables(jnp.full((SAMPLE_ROWS,), PAST_LEN, jnp.int32))

    def residual_ln(a_p, a_s, w, xp, xs, g, b, *, bm, bn, name):
        (y_p,), (y_s,) = proj(a_p, a_s, w, bm=bm, bn=bn, out_dtypes=(F32,), resid=(xp, xs), name=name)
        xp, xp_b = layer_norm(y_p, g, b, bm=256, name="ln_prompt")
        xs, xs_b = layer_norm(y_s, g, b, bm=SAMPLE_ROWS, name="ln_sample")
        return xp, xp_b, xs, xs_b

    n_a_layers = (DEPTH + 1) // 2
    qkv_a_layers = []
    a_sample = [[] for _ in range(N_GROUPS)]
    bk_p, bv_p, bk_s, bv_s = [], [], [], []

    for i in range(DEPTH):
        j = i // 2
        if i % 2 == 0:
            (qkv_p,), (qkv_s,) = proj(xp_b, xs_b, wqa[j], bm=1024, bn=768, out_dtypes=(F32,),
                                      rope=(tabs_p, tabs_s, 2 * N_GROUPS * hw), name="qkv_a")
            qkv_a_layers.append(qkv_p)
            att_p = attn_a_prompt(qkv_p, batch, seq)
            att_s = attn_a_sample(qkv_s, caches_a, j)
            att_s = _pad_rows(att_s.reshape(nb, hw), SAMPLE_ROWS).astype(BF16)
            for g, (window, _) in enumerate(DIL_GROUPS):
                kcol, vcol = (N_GROUPS + g) * hw, (2 * N_GROUPS + g) * hw
                kn = qkv_s[:nb, kcol:kcol + hw].reshape(nb, 1, H_A, HEAD_DIM)
                vn = qkv_s[:nb, vcol:vcol + hw].reshape(nb, 1, H_A, HEAD_DIM)
                a_sample[g].append(jnp.concatenate([kn, vn], axis=1))
            xp, xp_b, xs, xs_b = residual_ln(att_p, att_s, woa[j], xp, xs, ln_mix_g[i], ln_mix_b[i],
                                             bm=1024, bn=1024, name="o_a")
        else:
            secs = [proj(xp_b, xs_b, wqb[j], col_off=sec * dm, n_cols=dm, bm=1024, bn=1024,
                         out_dtypes=(F32, BF16), name="qkv_b") for sec in range(3)]
            ((_, q_pb), (q_s, _)), ((k_p, k_pb), (k_s, _)), ((v_p, v_pb), (v_s, _)) = secs
            att_p = attn_b_prompt(q_pb, k_pb, v_pb, sb_bias[j], batch, seq)
            att_s = attn_b_sample(q_s, sb_bias[j], cache_b_k, cache_b_v, page_table, j)
            att_s = _pad_rows(att_s.reshape(nb, dm), SAMPLE_ROWS).astype(BF16)
            bk_p.append(k_p.reshape(batch, seq, H_B, HEAD_DIM))
            bv_p.append(v_p.reshape(batch, seq, H_B, HEAD_DIM))
            bk_s.append(k_s[:nb].reshape(nb, 1, H_B, HEAD_DIM))
            bv_s.append(v_s[:nb].reshape(nb, 1, H_B, HEAD_DIM))
            xp, xp_b, xs, xs_b = residual_ln(att_p, att_s, wob[j], xp, xs, ln_mix_g[i], ln_mix_b[i],
                                             bm=1024, bn=512, name="o_b")
        act_p, act_s = swiglu_in(xp_b, xs_b, w_ffn_in, i, bm=1024, bn=256, name="ffn_in")
        xp, xp_b, xs, xs_b = residual_ln(act_p, act_s, wout[i], xp, xs, ln_ffn_g[i], ln_ffn_b[i],
                                         bm=512, bn=512, name="ffn_out")

    y_prompt = xp.reshape(batch, seq, dm)
    y_sample = xs[:nb].reshape(nb, 1, dm)
    st = lambda parts: jnp.stack(parts, axis=0)
    shifted = cache_shift([c.reshape((n_a_layers * nb,) + c.shape[2:]) for c in caches_a],
                          [jnp.concatenate(a_sample[g], axis=0) for g in range(N_GROUPS)])
    new_a_sample = [s.reshape(c.shape) for s, c in zip(shifted, caches_a)]
    new_a_prompt = [_from_storage_rows(kv_interleave(qkv_a_layers, g, batch, seq, min(window, seq)),
                                       (n_a_layers, batch, min(window, seq)))
                    for g, (window, _) in enumerate(DIL_GROUPS)]
    return (y_prompt, y_sample,
            new_a_prompt[0], new_a_sample[0], new_a_prompt[1], new_a_sample[1], new_a_prompt[2], new_a_sample[2],
            st(bk_p), st(bv_p), st(bk_s), st(bv_s))
```

```python
import functools
import math

import jax
import jax.numpy as jnp
from jax import lax
from jax.experimental import pallas as pl
from jax.experimental.pallas import tpu as pltpu

F32 = jnp.float32
BF16 = jnp.bfloat16

D_MODEL = 4096
DEPTH = 4
PAST_LEN = 8192
PAGE_SIZE = 128
HEAD_DIM = 128
DIL_GROUPS = ((128, 1), (512, 4), (2048, 16))
N_GROUPS = len(DIL_GROUPS)
H_A = 10
H_B = D_MODEL // HEAD_DIM
ROT_DIM = HEAD_DIM // 4
ROPE_THETA = 500000.0
D_FF = -(-8 * D_MODEL // (3 * 256)) * 256
QB = 128
ALPHA = (2.0 * DEPTH) ** 0.25
LN_EPS = 1e-5
SCALE = 1.0 / math.sqrt(HEAD_DIM)

LANES = 128
SUBLANES = 8
SAMPLE_ROWS = 16
VMEM_LIMIT = 56 * 1024 * 1024


def _cparams(sem):
    return pltpu.CompilerParams(dimension_semantics=sem, vmem_limit_bytes=VMEM_LIMIT)


def _dot(a, b):
    return jnp.dot(a, b, preferred_element_type=F32)


def _dot_nt(a, b):
    return lax.dot_general(a, b, (((1,), (1,)), ((), ())), preferred_element_type=F32)


def _rope(x, c, s1, s2):
    return x * c + pltpu.roll(x, ROT_DIM // 2, 1) * s1 + pltpu.roll(x, LANES - ROT_DIM // 2, 1) * s2


def _proj_kernel(n_row_tiles, n_rope_tiles, has_resid, n_out, *refs):
    xp_ref, xs_ref, w_ref = refs[:3]
    pos = 3
    if has_resid:
        rp_ref, rs_ref = refs[pos:pos + 2]
        pos += 2
    if n_rope_tiles:
        tabs_p, tabs_s = refs[pos:pos + 3], refs[pos + 3:pos + 6]
        pos += 6
    outs_p, outs_s = refs[pos:pos + n_out], refs[pos + n_out:pos + 2 * n_out]
    j = pl.program_id(0)
    i = pl.program_id(1)

    def emit(x_ref, r_ref, tabs, outs):
        y = _dot(x_ref[...], w_ref[...])
        if has_resid:
            y = ALPHA * r_ref[...] + y

        def plain():
            for o in outs:
                o[...] = y.astype(o.dtype)

        if not n_rope_tiles:
            plain()
            return

        @pl.when(j < n_rope_tiles)
        def _():
            c, s1, s2 = tabs[0][...], tabs[1][...], tabs[2][...]
            for t in range(y.shape[1] // LANES):
                r = _rope(y[:, t * LANES:(t + 1) * LANES], c, s1, s2)
                for o in outs:
                    o[:, t * LANES:(t + 1) * LANES] = r.astype(o.dtype)

        pl.when(j >= n_rope_tiles)(plain)

    @pl.when(i < n_row_tiles)
    def _():
        emit(xp_ref, rp_ref if has_resid else None, tabs_p if n_rope_tiles else None, outs_p)

    @pl.when(i == n_row_tiles)
    def _():
        emit(xs_ref, rs_ref if has_resid else None, tabs_s if n_rope_tiles else None, outs_s)


def proj(xp, xs, w, *, bm, bn, out_dtypes, name, col_off=0, n_cols=None, resid=None, rope=None):
    mp, kd = xp.shape
    ms = xs.shape[0]
    n_cols = w.shape[1] if n_cols is None else n_cols
    nI, nJ = mp // bm, n_cols // bn
    assert nI * bm == mp and nJ * bn == n_cols and col_off % bn == 0 and bn % LANES == 0
    joff = col_off // bn
    row_p = lambda j, i: (jnp.minimum(i, nI - 1), 0)
    tile_p = lambda j, i: (jnp.minimum(i, nI - 1), j)
    fixed = lambda j, i: (0, 0)
    tile_s = lambda j, i: (0, j)
    args = [xp, xs, w]
    in_specs = [pl.BlockSpec((bm, kd), row_p), pl.BlockSpec((ms, kd), fixed),
                pl.BlockSpec((kd, bn), lambda j, i: (0, j + joff))]
    if resid is not None:
        args += list(resid)
        in_specs += [pl.BlockSpec((bm, bn), tile_p), pl.BlockSpec((ms, bn), tile_s)]
    n_rope_tiles = 0
    if rope is not None:
        tabs_p, tabs_s, n_rope_cols = rope
        assert n_rope_cols % bn == 0
        n_rope_tiles = n_rope_cols // bn
        args += list(tabs_p) + list(tabs_s)
        in_specs += [pl.BlockSpec((bm, LANES), row_p)] * 3 + [pl.BlockSpec((ms, LANES), fixed)] * 3
    n_out = len(out_dtypes)
    outs = pl.pallas_call(
        functools.partial(_proj_kernel, nI, n_rope_tiles, resid is not None, n_out),
        grid=(nJ, nI + 1),
        in_specs=in_specs,
        out_specs=[pl.BlockSpec((bm, bn), tile_p)] * n_out + [pl.BlockSpec((ms, bn), tile_s)] * n_out,
        out_shape=[jax.ShapeDtypeStruct((mp, n_cols), dt) for dt in out_dtypes]
                  + [jax.ShapeDtypeStruct((ms, n_cols), dt) for dt in out_dtypes],
        compiler_params=_cparams(("arbitrary", "arbitrary")),
        name=name,
    )(*args)
    return outs[:n_out], outs[n_out:]


W_CAST_ROWS = 512


def _swiglu_kernel(n_row_tiles, xp_ref, xs_ref, wg_ref, wu_ref, yp_ref, ys_ref, wgb_scr, wub_scr):
    i = pl.program_id(1)

    @pl.when(i == 0)
    def _():
        for r in range(0, wg_ref.shape[0], W_CAST_ROWS):
            wgb_scr[r:r + W_CAST_ROWS, :] = wg_ref[r:r + W_CAST_ROWS, :].astype(BF16)
            wub_scr[r:r + W_CAST_ROWS, :] = wu_ref[r:r + W_CAST_ROWS, :].astype(BF16)

    def act(x):
        g = _dot(x, wgb_scr[...])
        u = _dot(x, wub_scr[...])
        return ((g * (1.0 / (1.0 + jnp.exp(-g)))) * u).astype(yp_ref.dtype)

    @pl.when(i < n_row_tiles)
    def _():
        yp_ref[...] = act(xp_ref[...])

    @pl.when(i == n_row_tiles)
    def _():
        ys_ref[...] = act(xs_ref[...])


def swiglu_in(xp, xs, w_in, layer, *, bm, bn, name):
    mp, kd = xp.shape
    ms = xs.shape[0]
    ff = w_in.shape[2] // 2
    nI, nJ = mp // bm, ff // bn
    assert nI * bm == mp and nJ * bn == ff and kd % W_CAST_ROWS == 0
    return pl.pallas_call(
        functools.partial(_swiglu_kernel, nI),
        grid=(nJ, nI + 1),
        in_specs=[
            pl.BlockSpec((bm, kd), lambda j, i: (jnp.minimum(i, nI - 1), 0)),
            pl.BlockSpec((ms, kd), lambda j, i: (0, 0)),
            pl.BlockSpec((None, kd, bn), lambda j, i: (layer, 0, j)),
            pl.BlockSpec((None, kd, bn), lambda j, i: (layer, 0, j + nJ)),
        ],
        out_specs=[
            pl.BlockSpec((bm, bn), lambda j, i: (jnp.minimum(i, nI - 1), j)),
            pl.BlockSpec((ms, bn), lambda j, i: (0, j)),
        ],
        out_shape=[jax.ShapeDtypeStruct((mp, ff), BF16), jax.ShapeDtypeStruct((ms, ff), BF16)],
        scratch_shapes=[pltpu.VMEM((kd, bn), BF16), pltpu.VMEM((kd, bn), BF16)],
        compiler_params=_cparams(("arbitrary", "arbitrary")),
        name=name,
    )(xp, xs, w_in, w_in)


LN_CHUNK = 16


def _ln_kernel(y_ref, g_ref, b_ref, of_ref, ob_ref):
    g = g_ref[...]
    b = b_ref[...]

    def body(c, carry):
        sl = pl.ds(pl.multiple_of(c * LN_CHUNK, LN_CHUNK), LN_CHUNK)
        y = y_ref[sl, :]
        mu = jnp.mean(y, axis=-1, keepdims=True)
        yc = y - mu
        var = jnp.mean(yc * yc, axis=-1, keepdims=True)
        out = yc * lax.rsqrt(var + LN_EPS) * g + b
        of_ref[sl, :] = out
        ob_ref[sl, :] = out.astype(BF16)
        return carry

    lax.fori_loop(0, y_ref.shape[0] // LN_CHUNK, body, 0)


def layer_norm(y, g, b, *, bm, name):
    m, dm = y.shape
    assert m % bm == 0 and bm % LN_CHUNK == 0
    row = pl.BlockSpec((bm, dm), lambda i: (i, 0))
    vec = pl.BlockSpec((1, dm), lambda i: (0, 0))
    return pl.pallas_call(
        _ln_kernel,
        grid=(m // bm,),
        in_specs=[row, vec, vec],
        out_specs=[row, row],
        out_shape=[jax.ShapeDtypeStruct((m, dm), F32), jax.ShapeDtypeStruct((m, dm), BF16)],
        compiler_params=_cparams(("arbitrary",)),
        name=name,
    )(y, g.reshape(1, dm), b.reshape(1, dm))


A_BATCH = 4
A_BLOCKS = 3


def _attn_a_prompt_kernel(seq, q0, q1, q2, k0, k1, k2, v0, v1, v2, o_ref, os0, os1, os2, ls0, ls1, ls2):
    qs, ks, vs = (q0, q1, q2), (k0, k1, k2), (v0, v1, v2)
    o_scr, l_scr = (os0, os1, os2), (ls0, ls1, ls2)
    qq = lax.broadcasted_iota(jnp.int32, (QB, QB), 0)
    kk = lax.broadcasted_iota(jnp.int32, (QB, QB), 1)
    causal = kk <= qq
    qq2 = lax.broadcasted_iota(jnp.int32, (QB, 2 * QB), 0)
    kk2 = lax.broadcasted_iota(jnp.int32, (QB, 2 * QB), 1)
    band = (kk2 >= qq2) & (kk2 <= qq2 + QB)

    def rows(start, n, d):
        return pl.ds(start, n) if d == 1 else pl.ds(start, n, stride=d)

    def emit(g, blocks, mask):
        s = [_dot_nt(qs[g][qr, :].astype(BF16), ks[g][kr, :].astype(BF16)) for qr, kr in blocks]
        p, m, l = [], [], []
        for x in s:
            x = jnp.where(mask, x * SCALE, -jnp.inf)
            mx = jnp.max(x, axis=-1, keepdims=True)
            e = jnp.exp(x - mx)
            m.append(mx)
            p.append(e.astype(BF16))
            l.append(jnp.sum(e, axis=-1, keepdims=True))
        o = [_dot(p[n], vs[g][kr, :].astype(BF16)) for n, (_, kr) in enumerate(blocks)]
        for n, (qr, _) in enumerate(blocks):
            o_scr[g][qr, :] = o[n] / l[n]
            l_scr[g][qr, :] = jnp.broadcast_to(m[n] + jnp.log(l[n]), (QB, LANES))

    for g, (window, d) in enumerate(DIL_GROUPS):
        assert window // d == QB
        sub_len = seq // d
        n_blk = sub_len // QB
        for r0 in range(0, d, A_BATCH):
            rs = range(r0, min(r0 + A_BATCH, d))
            emit(g, [(rows(r, QB, d), rows(r, QB, d)) for r in rs], causal)
        if n_blk == 1:
            continue
        if d >= A_BATCH:
            for r0 in range(0, d, A_BATCH):
                def body(c, carry, g=g, r0=r0, d=d):
                    start = pl.multiple_of(c * (QB * d), QB * d)
                    emit(g, [(rows(start + r, QB, d), rows(start - QB * d + r, 2 * QB, d))
                             for r in range(r0, r0 + A_BATCH)], band)
                    return carry
                lax.fori_loop(1, n_blk, body, 0)
        else:
            assert d == 1 and (n_blk - 1) % A_BLOCKS == 0
            def body(t, carry, g=g):
                start = pl.multiple_of((1 + t * A_BLOCKS) * QB, QB)
                emit(g, [(rows(start + n * QB, QB, 1), rows(start + (n - 1) * QB, 2 * QB, 1))
                         for n in range(A_BLOCKS)], band)
                return carry
            lax.fori_loop(0, (n_blk - 1) // A_BLOCKS, body, 0)

    def merge(c, carry):
        sl = pl.ds(pl.multiple_of(c * QB, QB), QB)
        l0, l1, l2 = ls0[sl, :], ls1[sl, :], ls2[sl, :]
        mx = jnp.maximum(jnp.maximum(l0, l1), l2)
        w0, w1, w2 = jnp.exp(l0 - mx), jnp.exp(l1 - mx), jnp.exp(l2 - mx)
        den = w0 + w1 + w2
        out = (w0 / den) * os0[sl, :] + (w1 / den) * os1[sl, :] + (w2 / den) * os2[sl, :]
        o_ref[sl, :] = out.astype(o_ref.dtype)
        return carry

    lax.fori_loop(0, seq // QB, merge, 0)


def attn_a_prompt(qkv, batch, seq):
    def spec(sec, g):
        base = (sec * N_GROUPS + g) * H_A
        return pl.BlockSpec((seq, HEAD_DIM), lambda b, h: (b, base + h))
    in_specs = [spec(sec, g) for sec in range(3) for g in range(N_GROUPS)]
    return pl.pallas_call(
        functools.partial(_attn_a_prompt_kernel, seq),
        grid=(batch, H_A),
        in_specs=in_specs,
        out_specs=pl.BlockSpec((seq, HEAD_DIM), lambda b, h: (b, h)),
        out_shape=jax.ShapeDtypeStruct((batch * seq, H_A * HEAD_DIM), BF16),
        scratch_shapes=[pltpu.VMEM((seq, HEAD_DIM), F32)] * (2 * N_GROUPS),
        compiler_params=_cparams(("arbitrary", "arbitrary")),
        name="attn_a_prompt",
    )(*([qkv] * 9))


KV_ROWS = 2 * H_A
KV_SLOT = -(-KV_ROWS // SUBLANES) * SUBLANES


def _attn_a_sample_kernel(layer, n_seq, lens, qkv_ref, c0_ref, c1_ref, c2_ref, o_ref, b0, b1, b2, sems):
    b = pl.program_id(0)
    caches, bufs = (c0_ref, c1_ref, c2_ref), (b0, b1, b2)
    copies = []
    for g, (_, d) in enumerate(DIL_GROUPS):
        rows = lens[g] * KV_ROWS
        seq_rows = caches[g].at[pl.ds(pl.multiple_of((layer * n_seq + b) * rows, SUBLANES), rows), :]
        for m in range(QB):
            copies.append(pltpu.make_async_copy(seq_rows.at[pl.ds(m * d * KV_ROWS, KV_ROWS), :],
                                                bufs[g].at[pl.ds(m * KV_SLOT, KV_ROWS), :], sems.at[g]))
    for c in copies:
        c.start()
    for c in copies:
        c.wait()

    row = qkv_ref[pl.ds(b, 1), :]
    hw = H_A * HEAD_DIM

    def piece(sec, g, h):
        base = (sec * N_GROUPS + g) * hw + h * HEAD_DIM
        return row[:, base:base + HEAD_DIM]

    for h in range(H_A):
        outs, lses = [], []
        for g in range(N_GROUPS):
            q, kn, vn = piece(0, g, h), piece(1, g, h), piece(2, g, h)
            kc = bufs[g][pl.ds(2 * h, QB, stride=KV_SLOT), :]
            vc = bufs[g][pl.ds(2 * h + 1, QB, stride=KV_SLOT), :]
            sc = jnp.sum(kc * q, axis=-1, keepdims=True) * SCALE
            sn = jnp.sum(kn * q, axis=-1, keepdims=True) * SCALE
            m = jnp.maximum(jnp.max(sc, axis=0, keepdims=True), sn)
            pc = jnp.exp(sc - m)
            pn = jnp.exp(sn - m)
            l = jnp.sum(pc, axis=0, keepdims=True) + pn
            o = (jnp.sum(pc * vc, axis=0, keepdims=True) + pn * vn) / l
            outs.append(o)
            lses.append(m + jnp.log(l))
        mx = jnp.maximum(jnp.maximum(lses[0], lses[1]), lses[2])
        ws = [jnp.exp(l - mx) for l in lses]
        den = ws[0] + ws[1] + ws[2]
        out = (ws[0] / den) * outs[0] + (ws[1] / den) * outs[1] + (ws[2] / den) * outs[2]
        o_ref[0, :, h * HEAD_DIM:(h + 1) * HEAD_DIM] = out


def _storage_rows(cache):
    lead = cache.shape[:-3]
    perm = tuple(range(len(lead))) + (len(lead) + 1, len(lead), len(lead) + 2)
    return cache.transpose(perm).reshape(-1, HEAD_DIM)


def attn_a_sample(qkv_s, caches, layer):
    nb = caches[0].shape[1]
    lens = tuple(c.shape[2] for c in caches)
    assert all(l == QB * d for l, (_, d) in zip(lens, DIL_GROUPS))
    return pl.pallas_call(
        functools.partial(_attn_a_sample_kernel, layer, nb, lens),
        grid=(nb,),
        in_specs=[pl.BlockSpec(qkv_s.shape, lambda b: (0, 0))] + [pl.BlockSpec(memory_space=pl.ANY)] * N_GROUPS,
        out_specs=pl.BlockSpec((1, 1, H_A * HEAD_DIM), lambda b: (b, 0, 0)),
        out_shape=jax.ShapeDtypeStruct((nb, 1, H_A * HEAD_DIM), F32),
        scratch_shapes=[pltpu.VMEM((QB * KV_SLOT, HEAD_DIM), F32)] * N_GROUPS
                       + [pltpu.SemaphoreType.DMA((N_GROUPS,))],
        compiler_params=_cparams(("arbitrary",)),
        name="attn_a_sample",
    )(qkv_s, *[_storage_rows(c) for c in caches])


def _log_sigmoids(z):
    ls = jnp.minimum(z, 0.0) - jnp.log(1.0 + jnp.exp(-jnp.abs(z)))
    return ls, ls - z


def _split_bf16(x):
    hi = x.astype(BF16)
    lo = (x - hi.astype(F32)).astype(BF16)
    return hi, lo


def _later_keys_matrix(n):
    ji = lax.broadcasted_iota(jnp.int32, (2 * n, n), 0)
    si = lax.broadcasted_iota(jnp.int32, (2 * n, n), 1)
    return jnp.where((ji % n) > si, 1.0, 0.0).astype(BF16)


SB_T = 256
SB_HEADS = 4


def _attn_b_prompt_kernel(seq, bias_ref, q_ref, k_ref, v_ref, o_ref):
    g = pl.program_id(1)
    later = _later_keys_matrix(SB_T)
    qq = lax.broadcasted_iota(jnp.int32, (SB_T, SB_T), 0)
    kk = lax.broadcasted_iota(jnp.int32, (SB_T, SB_T), 1)
    before = kk < qq

    heads = range(SB_HEADS)
    cols = [slice(h * HEAD_DIM, (h + 1) * HEAD_DIM) for h in heads]

    def tiles(q0, k0, masked, state):
        z = [_dot_nt(q_ref[pl.ds(q0, SB_T), cols[h]], k_ref[pl.ds(k0, SB_T), cols[h]]) for h in heads]
        ls, lk, hilo = [], [], []
        for h in heads:
            s, k = _log_sigmoids(z[h] * SCALE + bias_ref[g * SB_HEADS + h])
            if masked:
                k = jnp.where(before, k, 0.0)
            ls.append(s)
            lk.append(k)
            hilo.append(jnp.concatenate(_split_bf16(k), axis=1))
        after = [_dot(hilo[h], later) for h in heads]
        a = []
        for h in heads:
            w = jnp.exp(ls[h] + after[h] + state[2 * h])
            if masked:
                w = jnp.where(before, w, 0.0)
            a.append(w.astype(BF16))
        out = []
        for h in heads:
            out.append(state[2 * h] + after[h][:, 0:1] + lk[h][:, 0:1])
            out.append(state[2 * h + 1] + _dot(a[h], v_ref[pl.ds(k0, SB_T), cols[h]]))
        return tuple(out)

    def q_body(qi, _):
        q0 = pl.multiple_of(qi * SB_T, SB_T)
        init = (jnp.zeros((SB_T, 1), F32), jnp.zeros((SB_T, HEAD_DIM), F32)) * SB_HEADS
        state = tiles(q0, q0, True, init)

        def k_body(t, st):
            return tiles(q0, pl.multiple_of((qi - 1 - t) * SB_T, SB_T), False, st)

        state = lax.fori_loop(0, qi, k_body, state)
        for h in heads:
            o_ref[pl.ds(q0, SB_T), cols[h]] = state[2 * h + 1].astype(o_ref.dtype)
        return 0

    lax.fori_loop(0, seq // SB_T, q_body, 0)


def attn_b_prompt(q, k, v, bias, batch, seq):
    n_heads = q.shape[1] // HEAD_DIM
    assert n_heads % SB_HEADS == 0 and seq % SB_T == 0
    spec = pl.BlockSpec((seq, SB_HEADS * HEAD_DIM), lambda b, g: (b, g))
    return pl.pallas_call(
        functools.partial(_attn_b_prompt_kernel, seq),
        grid=(batch, n_heads // SB_HEADS),
        in_specs=[pl.BlockSpec(memory_space=pltpu.SMEM), spec, spec, spec],
        out_specs=spec,
        out_shape=jax.ShapeDtypeStruct((batch * seq, n_heads * HEAD_DIM), BF16),
        compiler_params=_cparams(("arbitrary", "arbitrary")),
        name="attn_b_prompt",
    )(bias, q, k, v)


def _attn_b_sample_kernel(n_pages, n_heads, pt_ref, q_ref, bias_ref, k_ref, v_ref, o_ref,
                          qt_scr, acc_scr, carry_scr):
    b = pl.program_id(0)
    p = pl.program_id(1)
    page = PAGE_SIZE

    @pl.when(p == 0)
    def _():
        qrow = q_ref[pl.ds(b, 1), :]
        for h in range(n_heads):
            qt_scr[h:h + 1, :] = qrow[:, h * HEAD_DIM:(h + 1) * HEAD_DIM]
        acc_scr[...] = jnp.zeros_like(acc_scr)
        carry_scr[...] = jnp.zeros_like(carry_scr)

    shape3 = (page, n_heads, LANES)
    own = lax.broadcasted_iota(jnp.int32, shape3, 0) == lax.broadcasted_iota(jnp.int32, shape3, 2)
    s3 = jnp.sum(k_ref[...] * qt_scr[...][None], axis=-1, keepdims=True)
    z = jnp.sum(jnp.where(own, s3, 0.0), axis=0)
    z = z * SCALE + bias_ref[...]
    ls, lk = _log_sigmoids(z)
    hi, lo = _split_bf16(lk)
    after = _dot(jnp.concatenate([hi, lo], axis=1), _later_keys_matrix(page))
    a = jnp.exp(ls + after + carry_scr[...])
    carry_scr[...] += after[:, 0:1] + lk[:, 0:1]
    a3 = jnp.sum(jnp.where(own, a[None], 0.0), axis=-1, keepdims=True)
    acc_scr[...] += jnp.sum(a3 * v_ref[...], axis=0)

    @pl.when(p == n_pages - 1)
    def _():
        for h in range(n_heads):
            o_ref[0, :, h * HEAD_DIM:(h + 1) * HEAD_DIM] = acc_scr[h:h + 1, :]


def attn_b_sample(q_s, bias, cache_k, cache_v, page_table, layer):
    nb, n_pages = page_table.shape
    n_layers, n_pool, page, n_heads, dh = cache_k.shape
    dm = n_heads * dh
    assert page == PAGE_SIZE and page == LANES and dh == HEAD_DIM and n_heads % SUBLANES == 0
    page_spec = pl.BlockSpec((None, None, page, n_heads, dh),
                             lambda b, p, pt: (layer, pt[b * n_pages + (n_pages - 1 - p)], 0, 0, 0))
    grid_spec = pltpu.PrefetchScalarGridSpec(
        num_scalar_prefetch=1,
        grid=(nb, n_pages),
        in_specs=[pl.BlockSpec(q_s.shape, lambda b, p, pt: (0, 0)),
                  pl.BlockSpec((n_heads, 1), lambda b, p, pt: (0, 0)),
                  page_spec, page_spec],
        out_specs=pl.BlockSpec((1, 1, dm), lambda b, p, pt: (b, 0, 0)),
        scratch_shapes=[pltpu.VMEM((n_heads, dh), F32), pltpu.VMEM((n_heads, dh), F32),
                        pltpu.VMEM((n_heads, 1), F32)],
    )
    return pl.pallas_call(
        functools.partial(_attn_b_sample_kernel, n_pages, n_heads),
        grid_spec=grid_spec,
        out_shape=jax.ShapeDtypeStruct((nb, 1, dm), F32),
        compiler_params=_cparams(("arbitrary", "arbitrary")),
        name="attn_b_sample",
    )(page_table.reshape(-1), q_s, bias.reshape(n_heads, 1), cache_k, cache_v)


def _kv_interleave_kernel(n_layers, *refs):
    o_ref = refs[-1]
    layer = pl.program_id(0)
    for l in range(n_layers):
        k_ref, v_ref = refs[2 * l], refs[2 * l + 1]

        @pl.when(layer == l)
        def _():
            n = k_ref.shape[0]
            for h in range(H_A):
                cols = slice(h * HEAD_DIM, (h + 1) * HEAD_DIM)
                o_ref[pl.ds(2 * h, n, stride=KV_ROWS), :] = k_ref[:, cols]
                o_ref[pl.ds(2 * h + 1, n, stride=KV_ROWS), :] = v_ref[:, cols]


def kv_interleave(qkvs, g, n_seq, seq, keep, *, bt=128):
    assert keep % bt == 0 and seq % bt == 0
    hw = H_A * HEAD_DIM
    tiles = keep // bt
    first = (seq - keep) // bt
    n_layers = len(qkvs)

    def slab(sec):
        return pl.BlockSpec((bt, hw), lambda l, b, i: (b * (seq // bt) + first + i, sec * N_GROUPS + g))

    return pl.pallas_call(
        functools.partial(_kv_interleave_kernel, n_layers),
        grid=(n_layers, n_seq, tiles),
        in_specs=[slab(1), slab(2)] * n_layers,
        out_specs=pl.BlockSpec((bt * KV_ROWS, HEAD_DIM), lambda l, b, i: ((l * n_seq + b) * tiles + i, 0)),
        out_shape=jax.ShapeDtypeStruct((n_layers * n_seq * keep * KV_ROWS, HEAD_DIM), F32),
        compiler_params=_cparams(("arbitrary", "arbitrary", "arbitrary")),
        name="kv_interleave",
    )(*[q for q in qkvs for _ in range(2)])


def _from_storage_rows(rows, lead):
    n = len(lead)
    return rows.reshape(tuple(lead) + (H_A, 2, HEAD_DIM)).transpose(tuple(range(n)) + (n + 1, n, n + 2))


SHIFT_POS = 128
SHIFT_NEXT = SUBLANES * KV_ROWS // math.gcd(SUBLANES, KV_ROWS)


def _cache_shift_kernel(n_blk, cur_ref, nxt_ref, new_ref, o_ref):
    i = pl.program_id(1)
    r = cur_ref.shape[0]
    o_ref[0:r - KV_ROWS, :] = cur_ref[KV_ROWS:r, :]

    @pl.when(i < n_blk - 1)
    def _():
        o_ref[r - KV_ROWS:r, :] = nxt_ref[0:KV_ROWS, :]

    @pl.when(i == n_blk - 1)
    def _():
        o_ref[r - KV_ROWS:r, :] = new_ref[0:KV_ROWS, :]


def cache_shift(olds, news):
    n_seq = olds[0].shape[0]
    outs = []
    for old, new in zip(olds, news):
        length = old.shape[1]
        assert length % SHIFT_POS == 0
        n_blk = length // SHIFT_POS
        r = SHIFT_POS * KV_ROWS
        assert r % SHIFT_NEXT == 0
        old2 = _storage_rows(old)
        new2 = jnp.pad(_storage_rows(new).reshape(n_seq, KV_ROWS, HEAD_DIM),
                       ((0, 0), (0, KV_SLOT - KV_ROWS), (0, 0))).reshape(n_seq * KV_SLOT, HEAD_DIM)
        last_next = old2.shape[0] // SHIFT_NEXT - 1
        out = pl.pallas_call(
            functools.partial(_cache_shift_kernel, n_blk),
            grid=(n_seq, n_blk),
            in_specs=[
                pl.BlockSpec((r, HEAD_DIM), lambda s, i: (s * n_blk + i, 0)),
                pl.BlockSpec((SHIFT_NEXT, HEAD_DIM),
                             lambda s, i: (jnp.minimum((s * n_blk + i + 1) * (r // SHIFT_NEXT), last_next), 0)),
                pl.BlockSpec((KV_SLOT, HEAD_DIM), lambda s, i: (s, 0)),
            ],
            out_specs=pl.BlockSpec((r, HEAD_DIM), lambda s, i: (s * n_blk + i, 0)),
            out_shape=jax.ShapeDtypeStruct(old2.shape, F32),
            compiler_params=_cparams(("arbitrary", "arbitrary")),
            name="cache_shift",
        )(old2, old2, new2)
        outs.append(_from_storage_rows(out, (n_seq, length)))
    return outs


def _rope_tables(pos):
    half = ROT_DIM // 2
    inv_freq = ROPE_THETA ** (-jnp.arange(half, dtype=F32) / half)
    ang = pos.astype(F32)[:, None] * inv_freq[None, :]
    cos, sin = jnp.cos(ang), jnp.sin(ang)
    n = pos.shape[0]
    rest = HEAD_DIM - ROT_DIM
    c = jnp.concatenate([cos, cos, jnp.ones((n, rest), F32)], axis=1)
    s1 = jnp.concatenate([jnp.zeros((n, half), F32), sin, jnp.zeros((n, rest), F32)], axis=1)
    s2 = jnp.concatenate([-sin, jnp.zeros((n, half + rest), F32)], axis=1)
    return c, s1, s2


def _pad_rows(x, rows):
    return jnp.pad(x, ((0, rows - x.shape[0]), (0, 0)))


def kernel(x_prompt, x_sample, cache_a_g0, cache_a_g1, cache_a_g2, cache_b_k, cache_b_v, page_table, w_qkv_a, w_o_a, w_qkv_b, w_o_b, sb_bias, w_ffn_in, w_ffn_out, ln_mix_g, ln_mix_b, ln_ffn_g, ln_ffn_b):
    batch, seq, dm = x_prompt.shape
    nb, dec_seq, _ = x_sample.shape
    assert dec_seq == 1 and nb <= SAMPLE_ROWS
    mp = batch * seq
    caches_a = (cache_a_g0, cache_a_g1, cache_a_g2)
    hw = H_A * HEAD_DIM

    wqa = w_qkv_a.astype(BF16)
    woa = w_o_a.astype(BF16)
    wqb = w_qkv_b.astype(BF16)
    wob = w_o_b.astype(BF16)
    wout = w_ffn_out.astype(BF16)

    xp = x_prompt.reshape(mp, dm)
    xs = _pad_rows(x_sample.reshape(nb, dm), SAMPLE_ROWS)
    xp_b, xs_b = xp.astype(BF16), xs.astype(BF16)

    tabs_p = _rope_tables(jnp.tile(jnp.arange(seq), batch))
    tabs_s = _rope_tables(jnp.full((SAMPLE_ROWS,), PAST_LEN, jnp.int32))

    def residual_ln(a_p, a_s, w, xp, xs, g, b, *, bm, bn, name):
        (y_p,), (y_s,) = proj(a_p, a_s, w, bm=bm, bn=bn, out_dtypes=(F32,), resid=(xp, xs), name=name)
        xp, xp_b = layer_norm(y_p, g, b, bm=256, name="ln_prompt")
        xs, xs_b = layer_norm(y_s, g, b, bm=SAMPLE_ROWS, name="ln_sample")
        return xp, xp_b, xs, xs_b

    n_a_layers = (DEPTH + 1) // 2
    qkv_a_layers = []
    a_sample = [[] for _ in range(N_GROUPS)]
    bk_p, bv_p, bk_s, bv_s = [], [], [], []

    for i in range(DEPTH):
        j = i // 2
        if i % 2 == 0:
            (qkv_p,), (qkv_s,) = proj(xp_b, xs_b, wqa[j], bm=1024, bn=768, out_dtypes=(F32,),
                                      rope=(tabs_p, tabs_s, 2 * N_GROUPS * hw), name="qkv_a")
            qkv_a_layers.append(qkv_p)
            att_p = attn_a_prompt(qkv_p, batch, seq)
            att_s = attn_a_sample(qkv_s, caches_a, j)
            att_s = _pad_rows(att_s.reshape(nb, hw), SAMPLE_ROWS).astype(BF16)
            for g, (window, _) in enumerate(DIL_GROUPS):
                kcol, vcol = (N_GROUPS + g) * hw, (2 * N_GROUPS + g) * hw
                kn = qkv_s[:nb, kcol:kcol + hw].reshape(nb, 1, H_A, HEAD_DIM)
                vn = qkv_s[:nb, vcol:vcol + hw].reshape(nb, 1, H_A, HEAD_DIM)
                a_sample[g].append(jnp.concatenate([kn, vn], axis=1))
            xp, xp_b, xs, xs_b = residual_ln(att_p, att_s, woa[j], xp, xs, ln_mix_g[i], ln_mix_b[i],
                                             bm=1024, bn=1024, name="o_a")
        else:
            secs = [proj(xp_b, xs_b, wqb[j], col_off=sec * dm, n_cols=dm, bm=1024, bn=1024,
                         out_dtypes=(F32, BF16), name="qkv_b") for sec in range(3)]
            ((_, q_pb), (q_s, _)), ((k_p, k_pb), (k_s, _)), ((v_p, v_pb), (v_s, _)) = secs
            att_p = attn_b_prompt(q_pb, k_pb, v_pb, sb_bias[j], batch, seq)
            att_s = attn_b_sample(q_s, sb_bias[j], cache_b_k, cache_b_v, page_table, j)
            att_s = _pad_rows(att_s.reshape(nb, dm), SAMPLE_ROWS).astype(BF16)
            bk_p.append(k_p.reshape(batch, seq, H_B, HEAD_DIM))
            bv_p.append(v_p.reshape(batch, seq, H_B, HEAD_DIM))
            bk_s.append(k_s[:nb].reshape(nb, 1, H_B, HEAD_DIM))
            bv_s.append(v_s[:nb].reshape(nb, 1, H_B, HEAD_DIM))
            xp, xp_b, xs, xs_b = residual_ln(att_p, att_s, wob[j], xp, xs, ln_mix_g[i], ln_mix_b[i],
                                             bm=1024, bn=512, name="o_b")
        act_p, act_s = swiglu_in(xp_b, xs_b, w_ffn_in, i, bm=1024, bn=256, name="ffn_in")
        xp, xp_b, xs, xs_b = residual_ln(act_p, act_s, wout[i], xp, xs, ln_ffn_g[i], ln_ffn_b[i],
                                         bm=512, bn=512, name="ffn_out")

    y_prompt = xp.reshape(batch, seq, dm)
    y_sample = xs[:nb].reshape(nb, 1, dm)
    st = lambda parts: jnp.stack(parts, axis=0)
    shifted = cache_shift([c.reshape((n_a_layers * nb,) + c.shape[2:]) for c in caches_a],
                          [jnp.concatenate(a_sample[g], axis=0) for g in range(N_GROUPS)])
    new_a_sample = [s.reshape(c.shape) for s, c in zip(shifted, caches_a)]
    new_a_prompt = [_from_storage_rows(kv_interleave(qkv_a_layers, g, batch, seq, min(window, seq)),
                                       (n_a_layers, batch, min(window, seq)))
                    for g, (window, _) in enumerate(DIL_GROUPS)]
    return (y_prompt, y_sample,
            new_a_prompt[0], new_a_sample[0], new_a_prompt[1], new_a_sample[1], new_a_prompt[2], new_a_sample[2],
            st(bk_p), st(bv_p), st(bk_s), st(bv_s))
```
